```python
import math
import jax
import jax.numpy as jnp
from jax import lax
import numpy as np

D_MODEL = 2048
BATCH = 8
SEQ = 4096
DEPTH = 1
DEC_BATCH = 8
DEC_SEQ = 16
PAST_LEN = 4096

CHUNK = 64
N_META = 16
MIX_WIDTH = D_MODEL
DN_WIDTH = MIX_WIDTH // 2
DN_HEAD_DIM = 128
DN_HEADS = DN_WIDTH // DN_HEAD_DIM
QKV_DIM = 3 * DN_WIDTH
CONV_W = 4
S5_WIDTH = MIX_WIDTH - DN_WIDTH
S5_GROUP_CH = 16
S5_GROUPS = S5_WIDTH // S5_GROUP_CH
S5_STATE = 64
IN_PROJ_DIM = QKV_DIM + DN_WIDTH + 2 * DN_HEADS + S5_WIDTH
FFN_HIDDEN = ((8 * D_MODEL // 3 + 255) // 256) * 256
DEEPNORM_ALPHA = (2.0 * DEPTH) ** 0.25
DEEPNORM_BETA = (8.0 * DEPTH) ** -0.25
LN_EPS = 1e-5
RMS_EPS = 1e-6

kernel_name = 'hybrid_gdn_s5_streaming_encoder'


def _f32(t):
    return t.astype(jnp.float32)


def layer_norm(x, g, b):
    x = _f32(x)
    mu = jnp.mean(x, axis=-1, keepdims=True)
    xc = x - mu
    var = jnp.mean(xc * xc, axis=-1, keepdims=True)
    return xc * lax.rsqrt(var + LN_EPS) * _f32(g) + _f32(b)


def l2_normalize(x):
    return x * lax.rsqrt(jnp.sum(x * x, axis=-1, keepdims=True) + RMS_EPS)


def pad_front(t, n):
    return jnp.pad(t, ((0, 0), (n, 0)) + ((0, 0),) * (t.ndim - 2))


def gated_delta_rule(q, k, v, g, beta, s0, block):
    bsz, seqlen, nh, _ = q.shape
    dv = v.shape[-1]
    nb = seqlen // block

    def to_blocks(t):
        t = jnp.moveaxis(t, 2, 1)
        return t.reshape((bsz, nh, nb, block) + t.shape[3:])

    q, k, v, g, beta = (to_blocks(t) for t in (q, k, v, g, beta))
    gcum = jnp.cumsum(g, axis=-1)
    pos = jnp.arange(block)
    incl = pos[:, None] >= pos[None, :]
    strict = pos[:, None] > pos[None, :]
    diff = gcum[..., :, None] - gcum[..., None, :]
    decay = jnp.where(incl, jnp.exp(jnp.where(incl, diff, 0.0)), 0.0)
    kb = k * beta[..., None]
    a_mat = jnp.where(strict, jnp.einsum('bhnid,bhnjd->bhnij', kb, k) * decay, 0.0)
    rhs = jnp.concatenate([v * beta[..., None], kb * jnp.exp(gcum)[..., None]], axis=-1)
    sol = lax.linalg.triangular_solve(a_mat, rhs, left_side=True, lower=True, unit_diagonal=True)
    w_val, k_cd = sol[..., :dv], sol[..., dv:]
    qk = jnp.einsum('bhnid,bhnjd->bhnij', q, k) * decay
    q_dec = q * jnp.exp(gcum)[..., None]
    k_dec = k * jnp.exp(gcum[..., -1:] - gcum)[..., None]
    g_tot = jnp.exp(gcum[..., -1])

    def step(s, xs):
        w_val_c, k_cd_c, qk_c, q_dec_c, k_dec_c, g_tot_c = xs
        w = w_val_c - jnp.einsum('bhcd,bhde->bhce', k_cd_c, s)
        o = jnp.einsum('bhcd,bhde->bhce', q_dec_c, s) + jnp.einsum('bhij,bhje->bhie', qk_c, w)
        s = s * g_tot_c[..., None, None] + jnp.einsum('bhcd,bhce->bhde', k_dec_c, w)
        return s, o

    xs = tuple(jnp.moveaxis(t, 2, 0) for t in (w_val, k_cd, qk, q_dec, k_dec, g_tot))
    s_final, o = lax.scan(step, s0, xs)
    o = jnp.moveaxis(o, 0, 2).reshape(bsz, nh, seqlen, dv)
    return jnp.moveaxis(o, 1, 2), s_final


def _complex_affine_combine(e1, e2):
    a1r, a1i, b1r, b1i = e1
    a2r, a2i, b2r, b2i = e2
    return (a2r * a1r - a2i * a1i,
            a2r * a1i + a2i * a1r,
            a2r * b1r - a2i * b1i + b2r,
            a2r * b1i + a2i * b1r + b2i)


def s5_ssm(u, x0_re, x0_im, a_re, a_im, log_dt, b_re, b_im, c_re, c_im, d):
    bsz, seqlen, _ = u.shape
    ug = u.reshape(bsz, seqlen, S5_GROUPS, S5_GROUP_CH)
    dt = jnp.exp(log_dt)[:, None]
    mag = jnp.exp(a_re * dt)
    lam_re = mag * jnp.cos(a_im * dt)
    lam_im = mag * jnp.sin(a_im * dt)
    den = a_re * a_re + a_im * a_im
    nr = lam_re - 1.0
    f_re = (nr * a_re + lam_im * a_im) / den
    f_im = (lam_im * a_re - nr * a_im) / den
    bb_re = f_re[..., None] * b_re - f_im[..., None] * b_im
    bb_im = f_re[..., None] * b_im + f_im[..., None] * b_re
    in_re = jnp.einsum('blgh,gph->blgp', ug, bb_re)
    in_im = jnp.einsum('blgh,gph->blgp', ug, bb_im)
    in_re = in_re.at[:, 0].add(lam_re * x0_re - lam_im * x0_im)
    in_im = in_im.at[:, 0].add(lam_re * x0_im + lam_im * x0_re)
    ar = jnp.broadcast_to(lam_re[None, None], (1, seqlen, S5_GROUPS, S5_STATE))
    ai = jnp.broadcast_to(lam_im[None, None], (1, seqlen, S5_GROUPS, S5_STATE))
    _, _, xr, xi = lax.associative_scan(_complex_affine_combine, (ar, ai, in_re, in_im), axis=1)
    y = jnp.einsum('blgp,ghp->blgh', xr, c_re) - jnp.einsum('blgp,ghp->blgh', xi, c_im)
    y = y.reshape(bsz, seqlen, S5_WIDTH) + d * u
    return y, xr[:, -1], xi[:, -1]


def hybrid_layer(h, conv_prev, s_delta0, s5_re0, s5_im0, n_pad, block,
                 w_in, conv_w, dn_a_log, dn_dt_bias, dn_norm_w,
                 s5_a_re, s5_a_im, s5_log_dt, s5_b_re, s5_b_im, s5_c_re, s5_c_im, s5_d,
                 s5_w_glu, s5_b_glu, w_out, ln1_g, ln1_b, ffn_w_gate, ffn_w_up, ffn_w_down,
                 ln2_g, ln2_b):
    h = _f32(h)
    bsz, seqlen, _ = h.shape
    proj = h @ _f32(w_in)
    qkv_pre, z, a_logit, b_logit, u = jnp.split(
        proj, [QKV_DIM, QKV_DIM + DN_WIDTH, QKV_DIM + DN_WIDTH + DN_HEADS,
               QKV_DIM + DN_WIDTH + 2 * DN_HEADS], axis=-1)

    cat = jnp.concatenate([_f32(conv_prev), qkv_pre], axis=1)
    cw = _f32(conv_w)
    qkv = sum(cat[:, i:i + seqlen] * cw[i] for i in range(CONV_W))
    qkv = jax.nn.silu(qkv)
    new_conv = cat[:, seqlen:]
    q, k, v = jnp.split(qkv, 3, axis=-1)
    q = l2_normalize(q.reshape(bsz, seqlen, DN_HEADS, DN_HEAD_DIM)) * (DN_HEAD_DIM ** -0.5)
    k = l2_normalize(k.reshape(bsz, seqlen, DN_HEADS, DN_HEAD_DIM))
    v = v.reshape(bsz, seqlen, DN_HEADS, DN_HEAD_DIM)
    beta = jax.nn.sigmoid(b_logit)
    g = -jnp.exp(_f32(dn_a_log)) * jax.nn.softplus(a_logit + _f32(dn_dt_bias))
    o, s_delta = gated_delta_rule(pad_front(q, n_pad), pad_front(k, n_pad), pad_front(v, n_pad),
                                  pad_front(g, n_pad), pad_front(beta, n_pad), _f32(s_delta0), block)
    o = o[:, n_pad:]
    o = o * lax.rsqrt(jnp.mean(o * o, axis=-1, keepdims=True) + RMS_EPS) * _f32(dn_norm_w)
    o = o * jax.nn.silu(z.reshape(bsz, seqlen, DN_HEADS, DN_HEAD_DIM))
    o = o.reshape(bsz, seqlen, DN_WIDTH)

    y5, s5_re, s5_im = s5_ssm(u, _f32(s5_re0), _f32(s5_im0), _f32(s5_a_re), _f32(s5_a_im),
                              _f32(s5_log_dt), _f32(s5_b_re), _f32(s5_b_im), _f32(s5_c_re),
                              _f32(s5_c_im), _f32(s5_d))
    y5 = jax.nn.gelu(y5)
    y5 = y5 * jax.nn.sigmoid(y5 @ _f32(s5_w_glu) + _f32(s5_b_glu))

    mix = jnp.concatenate([o, y5], axis=-1) @ _f32(w_out)
    x1 = layer_norm(DEEPNORM_ALPHA * h + mix, ln1_g, ln1_b)
    ffn = (jax.nn.silu(x1 @ _f32(ffn_w_gate)) * (x1 @ _f32(ffn_w_up))) @ _f32(ffn_w_down)
    x2 = layer_norm(DEEPNORM_ALPHA * x1 + ffn, ln2_g, ln2_b)
    return x2, new_conv, s_delta, s5_re, s5_im


def setup_inputs(seed: int = 0) -> dict:
    key = jax.random.key(seed)
    ks = jax.random.split(key, 32)

    def nrm(k, shape, scale):
        return scale * jax.random.normal(k, shape, jnp.float32)

    dt_dn = jnp.exp(jax.random.uniform(ks[12], (DEPTH, DN_HEADS), jnp.float32,
                                       minval=math.log(1e-3), maxval=math.log(1e-1)))
    n_idx = jnp.arange(S5_STATE, dtype=jnp.float32)
    return {
        'x_prompt': nrm(ks[0], (BATCH, SEQ, D_MODEL), 1.0),
        'x_sample': nrm(ks[1], (DEC_BATCH, DEC_SEQ, D_MODEL), 1.0),
        'state_conv_qkv': nrm(ks[2], (DEPTH, DEC_BATCH, CONV_W - 1, QKV_DIM), 1.0),
        'state_delta': nrm(ks[3], (DEPTH, DEC_BATCH, DN_HEADS, DN_HEAD_DIM, DN_HEAD_DIM), DN_HEAD_DIM ** -0.5),
        'state_s5_re': nrm(ks[4], (DEPTH, DEC_BATCH, S5_GROUPS, S5_STATE), 0.1),
        'state_s5_im': nrm(ks[5], (DEPTH, DEC_BATCH, S5_GROUPS, S5_STATE), 0.1),
        'meta_tokens': nrm(ks[6], (N_META, D_MODEL), 1.0),
        'ln_in_g': 1.0 + nrm(ks[7], (D_MODEL,), 0.02),
        'ln_in_b': nrm(ks[8], (D_MODEL,), 0.02),
        'w_in': nrm(ks[9], (DEPTH, D_MODEL, IN_PROJ_DIM), D_MODEL ** -0.5),
        'conv_w': nrm(ks[10], (DEPTH, CONV_W, QKV_DIM), 0.5),
        'dn_a_log': jnp.log(jax.random.uniform(ks[11], (DEPTH, DN_HEADS), jnp.float32, minval=1.0, maxval=16.0)),
        'dn_dt_bias': dt_dn + jnp.log(-jnp.expm1(-dt_dn)),
        'dn_norm_w': 1.0 + nrm(ks[13], (DEPTH, DN_HEAD_DIM), 0.02),
        's5_a_re': -0.5 + nrm(ks[14], (DEPTH, S5_GROUPS, S5_STATE), 0.01),
        's5_a_im': math.pi * n_idx + nrm(ks[15], (DEPTH, S5_GROUPS, S5_STATE), 0.01),
        's5_log_dt': jax.random.uniform(ks[16], (DEPTH, S5_GROUPS), jnp.float32,
                                        minval=math.log(1e-3), maxval=math.log(1e-1)),
        's5_b_re': nrm(ks[17], (DEPTH, S5_GROUPS, S5_STATE, S5_GROUP_CH), (2 * S5_GROUP_CH) ** -0.5),
        's5_b_im': nrm(ks[18], (DEPTH, S5_GROUPS, S5_STATE, S5_GROUP_CH), (2 * S5_GROUP_CH) ** -0.5),
        's5_c_re': nrm(ks[19], (DEPTH, S5_GROUPS, S5_GROUP_CH, S5_STATE), S5_STATE ** -0.5),
        's5_c_im': nrm(ks[20], (DEPTH, S5_GROUPS, S5_GROUP_CH, S5_STATE), S5_STATE ** -0.5),
        's5_d': nrm(ks[21], (DEPTH, S5_WIDTH), 1.0),
        's5_w_glu': nrm(ks[22], (DEPTH, S5_WIDTH, S5_WIDTH), S5_WIDTH ** -0.5),
        's5_b_glu': nrm(ks[23], (DEPTH, S5_WIDTH), 0.01),
        'w_out': nrm(ks[24], (DEPTH, MIX_WIDTH, D_MODEL), DEEPNORM_BETA * MIX_WIDTH ** -0.5),
        'ln1_g': 1.0 + nrm(ks[25], (DEPTH, D_MODEL), 0.02),
        'ln1_b': nrm(ks[26], (DEPTH, D_MODEL), 0.02),
        'ffn_w_gate': nrm(ks[27], (DEPTH, D_MODEL, FFN_HIDDEN), D_MODEL ** -0.5),
        'ffn_w_up': nrm(ks[28], (DEPTH, D_MODEL, FFN_HIDDEN), D_MODEL ** -0.5),
        'ffn_w_down': nrm(ks[29], (DEPTH, FFN_HIDDEN, D_MODEL), DEEPNORM_BETA * FFN_HIDDEN ** -0.5),
        'ln2_g': 1.0 + nrm(ks[30], (DEPTH, D_MODEL), 0.02),
        'ln2_b': nrm(ks[31], (DEPTH, D_MODEL), 0.02),
    }


def reference(x_prompt, x_sample, state_conv_qkv, state_delta, state_s5_re, state_s5_im,
              meta_tokens, ln_in_g, ln_in_b, w_in, conv_w, dn_a_log, dn_dt_bias, dn_norm_w,
              s5_a_re, s5_a_im, s5_log_dt, s5_b_re, s5_b_im, s5_c_re, s5_c_im, s5_d,
              s5_w_glu, s5_b_glu, w_out, ln1_g, ln1_b, ffn_w_gate, ffn_w_up, ffn_w_down,
              ln2_g, ln2_b):
    bp = x_prompt.shape[0]
    meta = jnp.broadcast_to(meta_tokens[None].astype(x_prompt.dtype), (bp, N_META, D_MODEL))
    hp = layer_norm(jnp.concatenate([meta, x_prompt], axis=1), ln_in_g, ln_in_b)
    hs = layer_norm(x_sample, ln_in_g, ln_in_b)
    n_pad = (-N_META) % CHUNK
    dec_block = x_sample.shape[1]
    conv_p, delta_p, re_p, im_p = [], [], [], []
    conv_s, delta_s, re_s, im_s = [], [], [], []
    for l in range(DEPTH):
        lw = (w_in[l], conv_w[l], dn_a_log[l], dn_dt_bias[l], dn_norm_w[l],
              s5_a_re[l], s5_a_im[l], s5_log_dt[l], s5_b_re[l], s5_b_im[l], s5_c_re[l], s5_c_im[l],
              s5_d[l], s5_w_glu[l], s5_b_glu[l], w_out[l], ln1_g[l], ln1_b[l],
              ffn_w_gate[l], ffn_w_up[l], ffn_w_down[l], ln2_g[l], ln2_b[l])
        hp, c_p, d_p, r_p, i_p = hybrid_layer(
            hp,
            jnp.zeros((bp, CONV_W - 1, QKV_DIM), jnp.float32),
            jnp.zeros((bp, DN_HEADS, DN_HEAD_DIM, DN_HEAD_DIM), jnp.float32),
            jnp.zeros((bp, S5_GROUPS, S5_STATE), jnp.float32),
            jnp.zeros((bp, S5_GROUPS, S5_STATE), jnp.float32),
            n_pad, CHUNK, *lw)
        hs, c_s, d_s, r_s, i_s = hybrid_layer(
            hs, state_conv_qkv[l], state_delta[l], state_s5_re[l], state_s5_im[l],
            0, dec_block, *lw)
        conv_p.append(c_p); delta_p.append(d_p); re_p.append(r_p); im_p.append(i_p)
        conv_s.append(c_s); delta_s.append(d_s); re_s.append(r_s); im_s.append(i_s)
    y_prompt = hp[:, N_META:].astype(x_prompt.dtype)
    y_sample = hs.astype(x_sample.dtype)
    new_conv_prompt = jnp.stack(conv_p, axis=0).astype(state_conv_qkv.dtype)
    new_delta_prompt = jnp.stack(delta_p, axis=0).astype(state_delta.dtype)
    new_s5_re_prompt = jnp.stack(re_p, axis=0).astype(state_s5_re.dtype)
    new_s5_im_prompt = jnp.stack(im_p, axis=0).astype(state_s5_im.dtype)
    new_conv_sample = jnp.stack(conv_s, axis=0).astype(state_conv_qkv.dtype)
    new_delta_sample = jnp.stack(delta_s, axis=0).astype(state_delta.dtype)
    new_s5_re_sample = jnp.stack(re_s, axis=0).astype(state_s5_re.dtype)
    new_s5_im_sample = jnp.stack(im_s, axis=0).astype(state_s5_im.dtype)
    return (y_prompt, y_sample, new_conv_prompt, new_delta_prompt, new_s5_re_prompt, new_s5_im_prompt,
            new_conv_sample, new_delta_sample, new_s5_re_sample, new_s5_im_sample)
```

```python
import functools
import math

import jax
import jax.numpy as jnp
from jax import lax
from jax.experimental import pallas as pl
from jax.experimental.pallas import tpu as pltpu

LN_EPS = 1e-5
RMS_EPS = 1e-6
CHUNK = 64
HEAD_DIM = 128
CONV_W = 4
S5_GROUP_CH = 16
GROUPS_PER_KTILE = 16
LANE = 128
SUBLANE = 8
VMEM_LIMIT = 56 * 1024 * 1024

F32 = jnp.float32
BF16 = jnp.bfloat16


def _dot(a, b):
    return jnp.dot(a.astype(BF16), b.astype(BF16), preferred_element_type=F32)


def _dot_nt(a, b):
    return lax.dot_general(a.astype(BF16), b.astype(BF16), (((1,), (1,)), ((), ())),
                           preferred_element_type=F32)


def _dot_tn(a, b):
    return lax.dot_general(a.astype(BF16), b.astype(BF16), (((0,), (0,)), ((), ())),
                           preferred_element_type=F32)


def _layer_norm(x, g, b):
    mu = jnp.mean(x, axis=-1, keepdims=True)
    xc = x - mu
    var = jnp.mean(xc * xc, axis=-1, keepdims=True)
    return xc * lax.rsqrt(var + LN_EPS) * g + b


def _sigmoid(x):
    return 1.0 / (1.0 + jnp.exp(-x))


def _silu(x):
    return x * _sigmoid(x)


def _softplus(x):
    return jnp.maximum(x, 0.0) + jnp.log1p(jnp.exp(-jnp.abs(x)))


def _ln_inproj_kernel(x_ref, g_ref, b_ref, w_ref, wab_ref, proj_ref, ab_ref, h_scr):
    @pl.when(pl.program_id(1) == 0)
    def _():
        h = _layer_norm(x_ref[...], g_ref[...], b_ref[...]).astype(BF16)
        h_scr[...] = h
        ab_ref[...] = jnp.dot(h, wab_ref[...], preferred_element_type=F32)

    proj_ref[...] = jnp.dot(h_scr[...], w_ref[...], preferred_element_type=F32)


def _ln_inproj(x2, ln_g, ln_b, w_main, w_ab, tm, tn):
    rows, d = x2.shape
    n = w_main.shape[1]
    return pl.pallas_call(
        _ln_inproj_kernel,
        grid=(rows // tm, n // tn),
        in_specs=[
            pl.BlockSpec((tm, d), lambda i, j: (i, 0)),
            pl.BlockSpec((1, d), lambda i, j: (0, 0)),
            pl.BlockSpec((1, d), lambda i, j: (0, 0)),
            pl.BlockSpec((d, tn), lambda i, j: (0, j)),
            pl.BlockSpec((d, LANE), lambda i, j: (0, 0)),
        ],
        out_specs=[
            pl.BlockSpec((tm, tn), lambda i, j: (i, j)),
            pl.BlockSpec((tm, LANE), lambda i, j: (i, 0)),
        ],
        out_shape=[jax.ShapeDtypeStruct((rows, n), F32), jax.ShapeDtypeStruct((rows, LANE), F32)],
        scratch_shapes=[pltpu.VMEM((tm, d), BF16)],
        compiler_params=pltpu.CompilerParams(
            dimension_semantics=("arbitrary", "arbitrary"), vmem_limit_bytes=VMEM_LIMIT),
        name="ln_inproj",
    )(x2, ln_g, ln_b, w_main, w_ab)


def _delta_kernel(qkv_ref, z_ref, ab_ref, cprev_ref, s0_ref, cw_ref, gate_ref, nw_ref,
                  o_ref, cnew_ref, sfin_ref, cbuf, s_scr, *, C, HP, H):
    t = pl.program_id(1)
    dn = H * HEAD_DIM
    R = HP * C
    SR = HP * HEAD_DIM
    tail = SUBLANE

    @pl.when(t == 0)
    def _():
        cbuf[0:tail, :] = cprev_ref[0]
        s_scr[...] = s0_ref[0]

    cbuf[tail:tail + C, :] = qkv_ref[0]

    def conv_cols(c0):
        acc = cbuf[tail - 3:tail - 3 + C, c0:c0 + HEAD_DIM] * cw_ref[0:1, c0:c0 + HEAD_DIM]
        for i in range(1, CONV_W):
            acc = acc + (cbuf[tail - 3 + i:tail - 3 + i + C, c0:c0 + HEAD_DIM]
                         * cw_ref[i:i + 1, c0:c0 + HEAD_DIM])
        return _silu(acc)

    def l2n(x):
        return x * lax.rsqrt(jnp.sum(x * x, axis=-1, keepdims=True) + RMS_EPS)

    ab = ab_ref[0]
    g_all = -jnp.exp(gate_ref[0:1, :]) * _softplus(ab + gate_ref[1:2, :])
    beta_all = _sigmoid(ab)
    rowc = lax.broadcasted_iota(jnp.int32, (C, LANE), 0)
    gcum_all = g_all
    s = 1
    while s < C:
        gcum_all = gcum_all + jnp.where(rowc >= s, pltpu.roll(gcum_all, s, 0), 0.0)
        s *= 2
    glast_all = gcum_all[C - 1:C, :]

    ri = lax.broadcasted_iota(jnp.int32, (R, R), 0)
    ci = lax.broadcasted_iota(jnp.int32, (R, R), 1)
    lc = int(math.log2(C))
    same = (ri >> lc) == (ci >> lc)
    eye = ri == ci
    incl = same & (ri >= ci)
    strict = same & (ri > ci)
    bi = lax.broadcasted_iota(jnp.int32, (R, SR), 0)
    bj = lax.broadcasted_iota(jnp.int32, (R, SR), 1)
    bdmask = (bi >> lc) == (bj >> int(math.log2(HEAD_DIM)))

    def block_diag(x):
        return jnp.where(bdmask, jnp.concatenate([x] * HP, axis=1), 0.0)

    for p in range(H // HP):
        heads = range(p * HP, (p + 1) * HP)
        qs = jnp.concatenate([l2n(conv_cols(h * HEAD_DIM)) * (HEAD_DIM ** -0.5) for h in heads], axis=0)
        ks = jnp.concatenate([l2n(conv_cols(dn + h * HEAD_DIM)) for h in heads], axis=0)
        vs = jnp.concatenate([conv_cols(2 * dn + h * HEAD_DIM) for h in heads], axis=0)
        beta_c = jnp.concatenate([beta_all[:, H + h:H + h + 1] for h in heads], axis=0)
        gc_c = jnp.concatenate([gcum_all[:, h:h + 1] for h in heads], axis=0)
        gl_c = jnp.concatenate(
            [jnp.broadcast_to(glast_all[:, h:h + 1], (C, 1)) for h in heads], axis=0)
        gc_r = jnp.sum(jnp.where(eye, gc_c, 0.0), axis=0, keepdims=True)

        decay = jnp.where(incl, jnp.exp(jnp.where(incl, gc_c - gc_r, 0.0)), 0.0)
        kb = ks * beta_c
        a_mat = jnp.where(strict, _dot_nt(kb, ks) * decay, 0.0)
        qk = _dot_nt(qs, ks) * decay

        x_inv = jnp.where(eye, 1.0, 0.0) - jnp.where(
            ((ri >> 1) == (ci >> 1)) & ((ri & 1) == 1) & ((ci & 1) == 0), a_mat, 0.0)
        s = 2
        while s < C:
            ls = int(math.log2(s))
            lower_left = (((ri >> (ls + 1)) == (ci >> (ls + 1)))
                          & (((ri >> ls) & 1) == 1) & (((ci >> ls) & 1) == 0))
            a_s = jnp.where(lower_left, a_mat, 0.0)
            x_inv = x_inv - _dot(_dot(x_inv, a_s), x_inv)
            s *= 2

        gexp = jnp.exp(gc_c)
        rhs = jnp.concatenate([vs * beta_c, kb * gexp], axis=1)
        sol = _dot(x_inv, rhs)
        w_val = sol[:, :HEAD_DIM]
        k_cd = sol[:, HEAD_DIM:]
        q_dec = qs * gexp
        k_dec = ks * jnp.exp(gl_c - gc_c)

        s_old = s_scr[p * SR:(p + 1) * SR, :]
        w = w_val - _dot(block_diag(k_cd), s_old)
        o = _dot(block_diag(q_dec), s_old) + _dot(qk, w)
        g_tot = jnp.concatenate(
            [jnp.broadcast_to(jnp.exp(glast_all[:, h:h + 1]), (HEAD_DIM, 1)) for h in heads], axis=0)
        s_scr[p * SR:(p + 1) * SR, :] = s_old * g_tot + _dot_tn(block_diag(k_dec), w)

        for i, h in enumerate(heads):
            oh = o[i * C:(i + 1) * C, :]
            oh = oh * lax.rsqrt(jnp.mean(oh * oh, axis=-1, keepdims=True) + RMS_EPS) * nw_ref[...]
            zz = z_ref[0, :, h * HEAD_DIM:(h + 1) * HEAD_DIM]
            o_ref[0, :, h * HEAD_DIM:(h + 1) * HEAD_DIM] = (oh * _silu(zz)).astype(o_ref.dtype)

    cbuf[0:tail, :] = cbuf[C:C + tail, :]
    cnew_ref[0] = cbuf[0:tail, :]

    @pl.when(t == pl.num_programs(1) - 1)
    def _():
        sfin_ref[0] = s_scr[...]


def _delta(proj3, ab3, conv_prev, s0, cw8, gate8, nw, C, HP):
    bsz, seqlen, _ = proj3.shape
    qkv_dim = cw8.shape[1]
    dn = qkv_dim // 3
    heads = dn // HEAD_DIM
    kern = functools.partial(_delta_kernel, C=C, HP=HP, H=heads)
    return pl.pallas_call(
        kern,
        grid=(bsz, seqlen // C),
        in_specs=[
            pl.BlockSpec((1, C, qkv_dim), lambda b, t: (b, t, 0)),
            pl.BlockSpec((1, C, dn), lambda b, t: (b, t, qkv_dim // dn)),
            pl.BlockSpec((1, C, LANE), lambda b, t: (b, t, 0)),
            pl.BlockSpec((1, SUBLANE, qkv_dim), lambda b, t: (b, 0, 0)),
            pl.BlockSpec((1, dn, HEAD_DIM), lambda b, t: (b, 0, 0)),
            pl.BlockSpec((SUBLANE, qkv_dim), lambda b, t: (0, 0)),
            pl.BlockSpec((SUBLANE, LANE), lambda b, t: (0, 0)),
            pl.BlockSpec((1, HEAD_DIM), lambda b, t: (0, 0)),
        ],
        out_specs=[
            pl.BlockSpec((1, C, dn), lambda b, t: (b, t, 0)),
            pl.BlockSpec((1, SUBLANE, qkv_dim), lambda b, t: (b, 0, 0)),
            pl.BlockSpec((1, dn, HEAD_DIM), lambda b, t: (b, 0, 0)),
        ],
        out_shape=[
            jax.ShapeDtypeStruct((bsz, seqlen, dn), BF16),
            jax.ShapeDtypeStruct((bsz, SUBLANE, qkv_dim), F32),
            jax.ShapeDtypeStruct((bsz, dn, HEAD_DIM), F32),
        ],
        scratch_shapes=[pltpu.VMEM((SUBLANE + C, qkv_dim), F32), pltpu.VMEM((dn, HEAD_DIM), F32)],
        compiler_params=pltpu.CompilerParams(
            dimension_semantics=("arbitrary", "arbitrary"), vmem_limit_bytes=VMEM_LIMIT),
        name="delta_rule",
    )(proj3, proj3, ab3, conv_prev, s0, cw8, gate8, nw)


def _s5_disc_kernel(are_ref, aim_ref, logdt_ref, lre_ref, lim_ref, fre_ref, fim_ref):
    a_re = are_ref[...]
    a_im = aim_ref[...]
    dt = jnp.exp(logdt_ref[...])
    mag = jnp.exp(a_re * dt)
    lam_re = mag * jnp.cos(a_im * dt)
    lam_im = mag * jnp.sin(a_im * dt)
    den = a_re * a_re + a_im * a_im
    nr = lam_re - 1.0
    lre_ref[...] = lam_re
    lim_ref[...] = lam_im
    fre_ref[...] = (nr * a_re + lam_im * a_im) / den
    fim_ref[...] = (lam_im * a_re - nr * a_im) / den


def _s5_disc(a_re, a_im, log_dt):
    shp = jax.ShapeDtypeStruct(a_re.shape, F32)
    return pl.pallas_call(_s5_disc_kernel, out_shape=[shp] * 4, name="s5_disc")(
        a_re, a_im, log_dt.reshape(-1, 1))


def _s5_kernel(u_ref, x0_ref, bb_ref, cc_ref, lam_ref, d_ref, wg_ref, bg_ref,
               y_ref, xfin_ref, utm, xs, ytm, st, *, B, TB, KT, SW):
    t = pl.program_id(0)
    kw = 2 * SW
    cw = u_ref.shape[2] // KT

    @pl.when(t == 0)
    def _():
        st[...] = x0_ref[...]

    nl = u_ref.shape[2] // LANE
    for b in range(B):
        for j in range(nl):
            utm[j, pl.ds(b, TB, stride=B), :] = u_ref[b, :, j * LANE:(j + 1) * LANE]
    u_t = jnp.concatenate([utm[j] for j in range(nl)], axis=1)
    ub = u_t.astype(BF16)
    for kt in range(KT):
        xs[:, kt * kw:(kt + 1) * kw] = jnp.dot(ub[:, kt * cw:(kt + 1) * cw], bb_ref[kt],
                                               preferred_element_type=F32)

    half = SW // 2
    for kt in range(KT):
        for hf in range(2):
            c_re = kt * kw + hf * half
            c_im = c_re + SW
            l_re = lam_ref[kt:kt + 1, hf * half:(hf + 1) * half]
            l_im = lam_ref[KT + kt:KT + kt + 1, hf * half:(hf + 1) * half]

            def step(tt, carry, c_re=c_re, c_im=c_im, l_re=l_re, l_im=l_im):
                x_re, x_im = carry
                r0 = pl.multiple_of(tt * B, B)
                n_re = l_re * x_re - l_im * x_im + xs[pl.ds(r0, B), c_re:c_re + half]
                n_im = l_re * x_im + l_im * x_re + xs[pl.ds(r0, B), c_im:c_im + half]
                xs[pl.ds(r0, B), c_re:c_re + half] = n_re
                xs[pl.ds(r0, B), c_im:c_im + half] = n_im
                return n_re, n_im

            f_re, f_im = lax.fori_loop(
                0, TB, step, (st[:, c_re:c_re + half], st[:, c_im:c_im + half]), unroll=4)
            st[:, c_re:c_re + half] = f_re
            st[:, c_im:c_im + half] = f_im

    y = jnp.concatenate(
        [jnp.dot(xs[:, kt * kw:(kt + 1) * kw].astype(BF16), cc_ref[kt], preferred_element_type=F32)
         for kt in range(KT)], axis=1) + d_ref[...] * u_t
    y = 0.5 * y * (1.0 + jnp.tanh(math.sqrt(2.0 / math.pi) * (y + 0.044715 * (y * y * y))))
    y = y * _sigmoid(jnp.dot(y.astype(BF16), wg_ref[...], preferred_element_type=F32) + bg_ref[...])
    for j in range(nl):
        ytm[j] = y[:, j * LANE:(j + 1) * LANE]
    for b in range(B):
        for j in range(nl):
            y_ref[b, :, j * LANE:(j + 1) * LANE] = ytm[j, pl.ds(b, TB, stride=B), :].astype(y_ref.dtype)

    @pl.when(t == pl.num_programs(0) - 1)
    def _():
        xfin_ref[...] = st[...]


def _s5(proj3, x0, bb, cc, lam, d_row, w_glu, b_glu, TB):
    bsz, seqlen, n = proj3.shape
    width = d_row.shape[1]
    kt, cw, kw = bb.shape
    sdim = x0.shape[1]
    const = lambda shape: pl.BlockSpec(shape, lambda t: (0,) * len(shape))
    kern = functools.partial(_s5_kernel, B=bsz, TB=TB, KT=kt, SW=kw // 2)
    return pl.pallas_call(
        kern,
        grid=(seqlen // TB,),
        in_specs=[
            pl.BlockSpec((bsz, TB, width), lambda t: (0, t, n // width - 1)),
            const((bsz, sdim)), const((kt, cw, kw)), const((kt, kw, cw)), const(lam.shape),
            const((1, width)), const((width, width)), const((1, width)),
        ],
        out_specs=[pl.BlockSpec((bsz, TB, width), lambda t: (0, t, 0)), const((bsz, sdim))],
        out_shape=[jax.ShapeDtypeStruct((bsz, seqlen, width), BF16),
                   jax.ShapeDtypeStruct((bsz, sdim), F32)],
        scratch_shapes=[pltpu.VMEM((width // LANE, bsz * TB, LANE), F32),
                        pltpu.VMEM((bsz * TB, sdim), F32),
                        pltpu.VMEM((width // LANE, bsz * TB, LANE), F32),
                        pltpu.VMEM((bsz, sdim), F32)],
        compiler_params=pltpu.CompilerParams(
            dimension_semantics=("arbitrary",), vmem_limit_bytes=VMEM_LIMIT),
        name="s5",
    )(proj3, x0, bb, cc, lam, d_row, w_glu, b_glu)


def _out_ln_kernel(x_ref, o_ref, y5_ref, lng_ref, lnb_ref, wout_ref, g1_ref, b1_ref, x1_ref,
                   *, alpha):
    dn = o_ref.shape[1]
    h = _layer_norm(x_ref[...], lng_ref[...], lnb_ref[...])
    mix = (jnp.dot(o_ref[...], wout_ref[0:dn, :], preferred_element_type=F32)
           + jnp.dot(y5_ref[...], wout_ref[dn:, :], preferred_element_type=F32))
    x1_ref[...] = _layer_norm(alpha * h + mix, g1_ref[...], b1_ref[...])


def _out_ln(x2, o2, y52, ln_g, ln_b, w_out, g1, b1, alpha, tm):
    rows, d = x2.shape
    row = lambda w: pl.BlockSpec((tm, w), lambda i: (i, 0))
    vec = pl.BlockSpec((1, d), lambda i: (0, 0))
    return pl.pallas_call(
        functools.partial(_out_ln_kernel, alpha=alpha),
        grid=(rows // tm,),
        in_specs=[row(d), row(o2.shape[1]), row(y52.shape[1]), vec, vec,
                  pl.BlockSpec(w_out.shape, lambda i: (0, 0)), vec, vec],
        out_specs=row(d),
        out_shape=jax.ShapeDtypeStruct((rows, d), F32),
        compiler_params=pltpu.CompilerParams(
            dimension_semantics=("arbitrary",), vmem_limit_bytes=VMEM_LIMIT),
        name="out_ln",
    )(x2, o2, y52, ln_g, ln_b, w_out, g1, b1)


def _ffn_kernel(x1_ref, wg_ref, wu_ref, wd_ref, g2_ref, b2_ref, y_ref, x1b, *, alpha):
    j = pl.program_id(1)

    @pl.when(j == 0)
    def _():
        x1 = x1_ref[...]
        x1b[...] = x1.astype(BF16)
        y_ref[...] = alpha * x1

    xb = x1b[...]
    gate = jnp.dot(xb, wg_ref[...], preferred_element_type=F32)
    up = jnp.dot(xb, wu_ref[...], preferred_element_type=F32)
    hb = (_silu(gate) * up).astype(BF16)
    th = wd_ref.shape[0]
    for c in range(0, y_ref.shape[1], th):
        y_ref[:, c:c + th] += jnp.dot(hb, wd_ref[:, c:c + th], preferred_element_type=F32)

    @pl.when(j == pl.num_programs(1) - 1)
    def _():
        y_ref[...] = _layer_norm(y_ref[...], g2_ref[...], b2_ref[...])


def _ffn(x1, w_gate, w_up, w_down, g2, b2, alpha, tm, th):
    rows, d = x1.shape
    fh = w_gate.shape[1]
    vec = pl.BlockSpec((1, d), lambda i, j: (0, 0))
    return pl.pallas_call(
        functools.partial(_ffn_kernel, alpha=alpha),
        grid=(rows // tm, fh // th),
        in_specs=[
            pl.BlockSpec((tm, d), lambda i, j: (i, 0)),
            pl.BlockSpec((d, th), lambda i, j: (0, j)),
            pl.BlockSpec((d, th), lambda i, j: (0, j)),
            pl.BlockSpec((th, d), lambda i, j: (j, 0)),
            vec, vec,
        ],
        out_specs=pl.BlockSpec((tm, d), lambda i, j: (i, 0)),
        out_shape=jax.ShapeDtypeStruct((rows, d), F32),
        scratch_shapes=[pltpu.VMEM((tm, d), BF16)],
        compiler_params=pltpu.CompilerParams(
            dimension_semantics=("arbitrary", "arbitrary"), vmem_limit_bytes=VMEM_LIMIT),
        name="ffn",
    )(x1, w_gate, w_up, w_down, g2, b2)


def _largest_divisor(n, cap, mult):
    best = mult
    for c in range(mult, cap + 1, mult):
        if n % c == 0:
            best = c
    return best


def _layer(x3, conv_prev8, s0, x0, prm, chunk, heads_per_pack, s5_tb):
    bsz, seqlen, d = x3.shape
    rows = bsz * seqlen
    x2 = x3.reshape(rows, d)
    tm = _largest_divisor(rows, 1024, 16)
    proj, ab = _ln_inproj(x2, prm["ln_in_g"], prm["ln_in_b"], prm["w_main"], prm["w_ab"], tm,
                          _largest_divisor(prm["w_main"].shape[1], 1024, LANE))
    proj3 = proj.reshape(bsz, seqlen, -1)
    o, conv_new, s_new = _delta(proj3, ab.reshape(bsz, seqlen, LANE), conv_prev8, s0,
                                prm["cw8"], prm["gate8"], prm["nw"], chunk, heads_per_pack)
    y5, x_new = _s5(proj3, x0, prm["bb"], prm["cc"], prm["lam"], prm["s5_d"], prm["w_glu"],
                    prm["b_glu"], s5_tb)
    x1 = _out_ln(x2, o.reshape(rows, -1), y5.reshape(rows, -1), prm["ln_in_g"], prm["ln_in_b"],
                 prm["w_out"], prm["ln1_g"], prm["ln1_b"], prm["alpha"],
                 _largest_divisor(rows, 512, 16))
    y = _ffn(x1, prm["w_gate"], prm["w_up"], prm["w_down"], prm["ln2_g"], prm["ln2_b"],
             prm["alpha"], tm, _largest_divisor(prm["w_gate"].shape[1], 256, LANE))
    return y.reshape(bsz, seqlen, d), conv_new, s_new, x_new


def kernel(x_prompt, x_sample, state_conv_qkv, state_delta, state_s5_re, state_s5_im, meta_tokens, ln_in_g, ln_in_b, w_in, conv_w, dn_a_log, dn_dt_bias, dn_norm_w, s5_a_re, s5_a_im, s5_log_dt, s5_b_re, s5_b_im, s5_c_re, s5_c_im, s5_d, s5_w_glu, s5_b_glu, w_out, ln1_g, ln1_b, ffn_w_gate, ffn_w_up, ffn_w_down, ln2_g, ln2_b):
    depth = w_in.shape[0]
    assert depth == 1, "single-layer trunk only"
    bp, seq, d = x_prompt.shape
    bs, dec_seq, _ = x_sample.shape
    n_meta = meta_tokens.shape[0]
    qkv_dim = conv_w.shape[-1]
    dn = qkv_dim // 3
    heads = dn // HEAD_DIM
    groups, sdim = s5_a_re.shape[1:]
    width = s5_d.shape[-1]
    assert n_meta == dec_seq and seq % CHUNK == 0 and width == groups * S5_GROUP_CH
    alpha = (2.0 * depth) ** 0.25
    row = lambda v: v.reshape(1, -1).astype(F32)

    w = w_in[0]
    n_gate = 2 * heads
    w_main = jnp.concatenate([w[:, :qkv_dim + dn], w[:, qkv_dim + dn + n_gate:]], axis=1).astype(BF16)
    w_ab = jnp.pad(w[:, qkv_dim + dn:qkv_dim + dn + n_gate], ((0, 0), (0, LANE - n_gate))).astype(BF16)
    gate8 = jnp.zeros((SUBLANE, LANE), F32)
    gate8 = gate8.at[0, :heads].set(dn_a_log[0]).at[1, :heads].set(dn_dt_bias[0])
    cw8 = jnp.pad(conv_w[0].astype(F32), ((0, SUBLANE - CONV_W), (0, 0)))

    lam_re, lam_im, f_re, f_im = _s5_disc(s5_a_re[0], s5_a_im[0], s5_log_dt[0])
    bb_re = f_re[..., None] * s5_b_re[0] - f_im[..., None] * s5_b_im[0]
    bb_im = f_re[..., None] * s5_b_im[0] + f_im[..., None] * s5_b_re[0]
    gk = GROUPS_PER_KTILE
    kt = groups // gk
    eye = jnp.eye(gk, dtype=F32)
    bb5 = jnp.stack([bb_re, bb_im]).reshape(2, kt, gk, sdim, S5_GROUP_CH)
    bb = jnp.einsum("rkgph,gG->kGhrgp", bb5, eye).reshape(kt, gk * S5_GROUP_CH, 2 * gk * sdim).astype(BF16)
    cc5 = jnp.stack([s5_c_re[0], -s5_c_im[0]]).reshape(2, kt, gk, S5_GROUP_CH, sdim)
    cc = jnp.einsum("rkghp,gG->krgpGh", cc5, eye).reshape(kt, 2 * gk * sdim, gk * S5_GROUP_CH).astype(BF16)
    lam = jnp.concatenate([lam_re.reshape(kt, gk * sdim), lam_im.reshape(kt, gk * sdim)], axis=0)

    prm = dict(
        ln_in_g=row(ln_in_g), ln_in_b=row(ln_in_b), w_main=w_main, w_ab=w_ab, cw8=cw8, gate8=gate8,
        nw=row(dn_norm_w[0]), bb=bb, cc=cc, lam=lam, s5_d=row(s5_d[0]), w_glu=s5_w_glu[0].astype(BF16),
        b_glu=row(s5_b_glu[0]), w_out=w_out[0].astype(BF16), ln1_g=row(ln1_g[0]), ln1_b=row(ln1_b[0]),
        w_gate=ffn_w_gate[0].astype(BF16), w_up=ffn_w_up[0].astype(BF16),
        w_down=ffn_w_down[0].astype(BF16), ln2_g=row(ln2_g[0]), ln2_b=row(ln2_b[0]), alpha=alpha)

    def s5_pack(re, im):
        b = re.shape[0]
        return jnp.stack([re.reshape(b, kt, gk, sdim), im.reshape(b, kt, gk, sdim)], axis=2).reshape(b, -1)

    def s5_unpack(x):
        b = x.shape[0]
        x = x.reshape(b, kt, 2, gk, sdim)
        return x[:, :, 0].reshape(b, groups, sdim), x[:, :, 1].reshape(b, groups, sdim)

    nb = 2 * SUBLANE
    assert bs + 1 <= nb
    pad_b = lambda a: jnp.pad(a.astype(F32), ((0, nb - a.shape[0]),) + ((0, 0),) * (a.ndim - 1))
    x_small = pad_b(jnp.concatenate([x_sample, meta_tokens[None].astype(x_sample.dtype)], axis=0))
    conv_small = pad_b(jnp.pad(state_conv_qkv[0], ((0, 0), (SUBLANE - CONV_W + 1, 0), (0, 0))))
    s0_small = pad_b(state_delta[0].reshape(bs, dn, HEAD_DIM))
    x0_small = pad_b(s5_pack(state_s5_re[0], state_s5_im[0]))
    y_s, conv_s, s_s, xf_s = _layer(x_small, conv_small, s0_small, x0_small, prm,
                                    dec_seq, heads, dec_seq)

    bc = lambda a: jnp.broadcast_to(a[bs:bs + 1], (bp,) + a.shape[1:])
    y_p, conv_p, s_p, xf_p = _layer(x_prompt, bc(conv_s), bc(s_s), bc(xf_s), prm,
                                    CHUNK, 256 // CHUNK, CHUNK)

    re_p, im_p = s5_unpack(xf_p)
    re_s, im_s = s5_unpack(xf_s[:bs])
    tail = SUBLANE - CONV_W + 1
    return (y_p.astype(x_prompt.dtype), y_s[:bs].astype(x_sample.dtype),
            conv_p[None, :, tail:].astype(state_conv_qkv.dtype),
            s_p.reshape(bp, heads, HEAD_DIM, HEAD_DIM)[None].astype(state_delta.dtype),
            re_p[None].astype(state_s5_re.dtype), im_p[None].astype(state_s5_im.dtype),
            conv_s[None, :bs, tail:].astype(state_conv_qkv.dtype),
            s_s[:bs].reshape(bs, heads, HEAD_DIM, HEAD_DIM)[None].astype(state_delta.dtype),
            re_s[None].astype(state_s5_re.dtype), im_s[None].astype(state_s5_im.dtype))
```

```python
import functools
import math

import jax
import jax.numpy as jnp
import numpy as np
from jax import lax
from jax.experimental import pallas as pl
from jax.experimental.pallas import tpu as pltpu

LN_EPS = 1e-5
RMS_EPS = 1e-6
CHUNK = 64
HEAD_DIM = 128
DELTA_CHUNKS_PER_STEP = 2
CONV_W = 4
S5_GROUP_CH = 16
GROUPS_PER_KTILE = 16
LANE = 128
SUBLANE = 8
VMEM_LIMIT = 56 * 1024 * 1024

F32 = jnp.float32
BF16 = jnp.bfloat16


def _layer_norm(x, g, b):
    mu = jnp.mean(x, axis=-1, keepdims=True)
    xc = x - mu
    var = jnp.mean(xc * xc, axis=-1, keepdims=True)
    return xc * lax.rsqrt(var + LN_EPS) * g + b


def _sigmoid(x):
    return 1.0 / (1.0 + jnp.exp(-x))


def _silu(x):
    return x * _sigmoid(x)


def _softplus(x):
    return jnp.maximum(x, 0.0) + jnp.log1p(jnp.exp(-jnp.abs(x)))


def _ln_inproj_kernel(x_ref, g_ref, b_ref, w_ref, wab_ref, proj_ref, ab_ref, h_scr):
    @pl.when(pl.program_id(1) == 0)
    def _():
        h = _layer_norm(x_ref[...], g_ref[...], b_ref[...]).astype(BF16)
        h_scr[...] = h
        ab_ref[...] = jnp.dot(h, wab_ref[...], preferred_element_type=F32)

    proj_ref[...] = jnp.dot(h_scr[...], w_ref[...], preferred_element_type=F32)


def _ln_inproj(x2, ln_g, ln_b, w_main, w_ab, tm, tn):
    rows, d = x2.shape
    n = w_main.shape[1]
    return pl.pallas_call(
        _ln_inproj_kernel,
        grid=(rows // tm, n // tn),
        in_specs=[
            pl.BlockSpec((tm, d), lambda i, j: (i, 0)),
            pl.BlockSpec((1, d), lambda i, j: (0, 0)),
            pl.BlockSpec((1, d), lambda i, j: (0, 0)),
            pl.BlockSpec((d, tn), lambda i, j: (0, j)),
            pl.BlockSpec((d, LANE), lambda i, j: (0, 0)),
        ],
        out_specs=[
            pl.BlockSpec((tm, tn), lambda i, j: (i, j)),
            pl.BlockSpec((tm, LANE), lambda i, j: (i, 0)),
        ],
        out_shape=[jax.ShapeDtypeStruct((rows, n), F32), jax.ShapeDtypeStruct((rows, LANE), F32)],
        scratch_shapes=[pltpu.VMEM((tm, d), BF16)],
        compiler_params=pltpu.CompilerParams(
            dimension_semantics=("arbitrary", "arbitrary"), vmem_limit_bytes=VMEM_LIMIT),
        name="ln_inproj",
    )(x2, ln_g, ln_b, w_main, w_ab)


def _delta_masks(C, HP):
    R, SR = HP * C, HP * HEAD_DIM
    ri = np.arange(R)[:, None]
    ci = np.arange(R)[None, :]
    same = (ri // C) == (ci // C)
    rr = [same & (ri >= ci)]
    s = 2
    while s < C:
        rr.append(((ri // (2 * s)) == (ci // (2 * s))) & ((ri // s) % 2 == 1) & ((ci // s) % 2 == 0))
        s *= 2
    first = ((ri // 2) == (ci // 2)) & (ri % 2 == 1) & (ci % 2 == 0)
    bd = (np.arange(2 * R)[:, None] % R) // C == (np.arange(SR)[None, :] // HEAD_DIM)
    return (jnp.asarray(np.stack(rr), BF16), jnp.asarray(np.stack([ri == ci, first]), F32),
            jnp.asarray(bd, BF16))


def _delta_kernel(qkv_ref, z_ref, ab_ref, cprev_ref, s0_ref, cw_ref, gate_ref, nw_ref,
                  mrr_ref, mf_ref, mbd_ref,
                  o_ref, cnew_ref, sfin_ref, cbuf, s_scr, *, C, NCH, HP, H):
    t = pl.program_id(1)
    dn = H * HEAD_DIM
    R = HP * C
    SR = HP * HEAD_DIM
    TB = NCH * C
    tail = SUBLANE

    @pl.when(t == 0)
    def _():
        cbuf[0:tail, :] = cprev_ref[0]
        s_scr[...] = s0_ref[0]

    cbuf[tail:tail + TB, :] = qkv_ref[0]

    def conv_cols(r0, c0):
        acc = cbuf[r0 + tail - 3:r0 + tail - 3 + C, c0:c0 + HEAD_DIM] * cw_ref[0:1, c0:c0 + HEAD_DIM]
        for i in range(1, CONV_W):
            acc = acc + (cbuf[r0 + tail - 3 + i:r0 + tail - 3 + i + C, c0:c0 + HEAD_DIM]
                         * cw_ref[i:i + 1, c0:c0 + HEAD_DIM])
        return _silu(acc)

    def l2n(x):
        return x * lax.rsqrt(jnp.sum(x * x, axis=-1, keepdims=True) + RMS_EPS)

    def block_diag(x, mask):
        return jnp.concatenate([x] * HP, axis=1) * mask

    bf = lambda x: x.astype(BF16)
    mm = lambda a, b: jnp.dot(a, b, preferred_element_type=F32)
    rowc = lax.broadcasted_iota(jnp.int32, (C, LANE), 0)
    reps = LANE // C
    lane_rep = lax.broadcasted_iota(jnp.int32, (1, LANE), 1) // C

    chains = [(c, p) for c in range(NCH) for p in range(H // HP)]
    gates = {}
    for c in range(NCH):
        r0 = c * C
        ab = ab_ref[0, r0:r0 + C, :]
        g_all = -jnp.exp(gate_ref[0:1, :]) * _softplus(ab + gate_ref[1:2, :])
        beta_all = _sigmoid(ab)
        gcum_all = g_all
        s = 1
        while s < C:
            gcum_all = gcum_all + jnp.where(rowc >= s, pltpu.roll(gcum_all, s, 0), 0.0)
            s *= 2
        gates[c] = (beta_all, gcum_all, gcum_all[C - 1:C, :],
                    jnp.concatenate([gcum_all] * reps, axis=0).T)

    st = {}
    for c, p in chains:
        r0 = c * C
        beta_all, gcum_all, glast_all, g_t = gates[c]
        heads = range(p * HP, (p + 1) * HP)
        qs = jnp.concatenate(
            [l2n(conv_cols(r0, h * HEAD_DIM)) * (HEAD_DIM ** -0.5) for h in heads], axis=0)
        ks = jnp.concatenate([l2n(conv_cols(r0, dn + h * HEAD_DIM)) for h in heads], axis=0)
        vs = jnp.concatenate([conv_cols(r0, 2 * dn + h * HEAD_DIM) for h in heads], axis=0)
        beta_c = jnp.concatenate([beta_all[:, H + h:H + h + 1] for h in heads], axis=0)
        gc_c = jnp.concatenate([gcum_all[:, h:h + 1] for h in heads], axis=0)
        gl_c = jnp.concatenate(
            [jnp.broadcast_to(glast_all[:, h:h + 1], (C, 1)) for h in heads], axis=0)
        row_blocks = []
        for m in range(R // LANE):
            blk = g_t[p * HP + m * reps:p * HP + m * reps + 1, :]
            for r in range(1, reps):
                hr = p * HP + m * reps + r
                blk = jnp.where(lane_rep == r, g_t[hr:hr + 1, :], blk)
            row_blocks.append(blk)
        gc_r = jnp.concatenate(row_blocks, axis=1)
        e = jnp.exp(jnp.minimum(gc_c - gc_r, 0.0))
        kb = ks * beta_c
        gexp = jnp.exp(gc_c)
        g_tot = jnp.concatenate(
            [jnp.broadcast_to(jnp.exp(glast_all[:, h:h + 1]), (HEAD_DIM, 1)) for h in heads], axis=0)
        st[c, p] = dict(
            e=e, lhs=bf(jnp.concatenate([kb, qs], axis=0)), ks_b=bf(ks),
            rhs=bf(jnp.concatenate([vs * beta_c, kb * gexp], axis=1)), q_dec=qs * gexp,
            kdec_bd=block_diag(bf(ks * jnp.exp(gl_c - gc_c)), mbd_ref[0:R, :]), g_tot=g_tot)

    for ch in chains:
        d = st[ch]
        d["kkqk"] = lax.dot_general(d["lhs"], d["ks_b"], (((1,), (1,)), ((), ())),
                                    preferred_element_type=F32)
    for ch in chains:
        d = st[ch]
        a_f = d["kkqk"][:R] * d["e"]
        d["a_b"] = bf(a_f)
        d["qk_b"] = bf(d["kkqk"][R:] * d["e"]) * mrr_ref[0]
        d["x"] = mf_ref[0] - a_f * mf_ref[1]
        d["x_b"] = bf(d["x"])
    for lvl in range(1, mrr_ref.shape[0]):
        for ch in chains:
            d = st[ch]
            d["p"] = mm(d["x_b"], d["a_b"] * mrr_ref[lvl])
        for ch in chains:
            d = st[ch]
            d["q"] = mm(bf(d["p"]), d["x_b"])
        for ch in chains:
            d = st[ch]
            d["x"] = d["x"] - d["q"]
            d["x_b"] = bf(d["x"])
    for ch in chains:
        d = st[ch]
        d["sol"] = mm(d["x_b"], d["rhs"])
    for ch in chains:
        d = st[ch]
        d["kq_bd"] = block_diag(
            bf(jnp.concatenate([d["sol"][:, HEAD_DIM:], d["q_dec"]], axis=0)), mbd_ref[...])

    packs = range(H // HP)
    for c in range(NCH):
        r0 = c * C
        s_old, ks_qs, w_b, o = {}, {}, {}, {}
        for p in packs:
            s_old[p] = s_scr[p * SR:(p + 1) * SR, :]
            ks_qs[p] = mm(st[c, p]["kq_bd"], bf(s_old[p]))
        for p in packs:
            w_b[p] = bf(st[c, p]["sol"][:, :HEAD_DIM] - ks_qs[p][:R])
            o[p] = mm(st[c, p]["qk_b"], w_b[p])
        for p in packs:
            s_scr[p * SR:(p + 1) * SR, :] = s_old[p] * st[c, p]["g_tot"] + lax.dot_general(
                st[c, p]["kdec_bd"], w_b[p], (((0,), (0,)), ((), ())), preferred_element_type=F32)
        for p in packs:
            o_p = ks_qs[p][R:] + o[p]
            for i in range(HP):
                h = p * HP + i
                oh = o_p[i * C:(i + 1) * C, :]
                oh = oh * lax.rsqrt(jnp.mean(oh * oh, axis=-1, keepdims=True) + RMS_EPS) * nw_ref[...]
                zz = z_ref[0, r0:r0 + C, h * HEAD_DIM:(h + 1) * HEAD_DIM]
                o_ref[0, r0:r0 + C, h * HEAD_DIM:(h + 1) * HEAD_DIM] = (oh * _silu(zz)).astype(o_ref.dtype)

    cbuf[0:tail, :] = cbuf[TB:TB + tail, :]
    cnew_ref[0] = cbuf[0:tail, :]

    @pl.when(t == pl.num_programs(1) - 1)
    def _():
        sfin_ref[0] = s_scr[...]


def _delta(proj3, ab3, conv_prev, s0, cw8, gate8, nw, C, NCH, HP):
    bsz, seqlen, _ = proj3.shape
    qkv_dim = cw8.shape[1]
    dn = qkv_dim // 3
    heads = dn // HEAD_DIM
    tb = NCH * C
    kern = functools.partial(_delta_kernel, C=C, NCH=NCH, HP=HP, H=heads)
    masks = _delta_masks(C, HP)
    const = lambda a: pl.BlockSpec(a.shape, lambda b, t: (0,) * a.ndim)
    return pl.pallas_call(
        kern,
        grid=(bsz, seqlen // tb),
        in_specs=[
            pl.BlockSpec((1, tb, qkv_dim), lambda b, t: (b, t, 0)),
            pl.BlockSpec((1, tb, dn), lambda b, t: (b, t, qkv_dim // dn)),
            pl.BlockSpec((1, tb, LANE), lambda b, t: (b, t, 0)),
            pl.BlockSpec((1, SUBLANE, qkv_dim), lambda b, t: (b, 0, 0)),
            pl.BlockSpec((1, dn, HEAD_DIM), lambda b, t: (b, 0, 0)),
            pl.BlockSpec((SUBLANE, qkv_dim), lambda b, t: (0, 0)),
            pl.BlockSpec((SUBLANE, LANE), lambda b, t: (0, 0)),
            pl.BlockSpec((1, HEAD_DIM), lambda b, t: (0, 0)),
            const(masks[0]), const(masks[1]), const(masks[2]),
        ],
        out_specs=[
            pl.BlockSpec((1, tb, dn), lambda b, t: (b, t, 0)),
            pl.BlockSpec((1, SUBLANE, qkv_dim), lambda b, t: (b, 0, 0)),
            pl.BlockSpec((1, dn, HEAD_DIM), lambda b, t: (b, 0, 0)),
        ],
        out_shape=[
            jax.ShapeDtypeStruct((bsz, seqlen, dn), BF16),
            jax.ShapeDtypeStruct((bsz, SUBLANE, qkv_dim), F32),
            jax.ShapeDtypeStruct((bsz, dn, HEAD_DIM), F32),
        ],
        scratch_shapes=[pltpu.VMEM((SUBLANE + tb, qkv_dim), F32), pltpu.VMEM((dn, HEAD_DIM), F32)],
        compiler_params=pltpu.CompilerParams(
            dimension_semantics=("arbitrary", "arbitrary"), vmem_limit_bytes=VMEM_LIMIT),
        name="delta_rule",
    )(proj3, proj3, ab3, conv_prev, s0, cw8, gate8, nw, *masks)


def _s5_disc_kernel(are_ref, aim_ref, logdt_ref, lre_ref, lim_ref, fre_ref, fim_ref):
    a_re = are_ref[...]
    a_im = aim_ref[...]
    dt = jnp.exp(logdt_ref[...])
    mag = jnp.exp(a_re * dt)
    lam_re = mag * jnp.cos(a_im * dt)
    lam_im = mag * jnp.sin(a_im * dt)
    den = a_re * a_re + a_im * a_im
    nr = lam_re - 1.0
    lre_ref[...] = lam_re
    lim_ref[...] = lam_im
    fre_ref[...] = (nr * a_re + lam_im * a_im) / den
    fim_ref[...] = (lam_im * a_re - nr * a_im) / den


def _s5_disc(a_re, a_im, log_dt):
    shp = jax.ShapeDtypeStruct(a_re.shape, F32)
    return pl.pallas_call(_s5_disc_kernel, out_shape=[shp] * 4, name="s5_disc")(
        a_re, a_im, log_dt.reshape(-1, 1))


def _s5_kernel(u_ref, x0_ref, bb_ref, cc_ref, lam_ref, d_ref, wg_ref, bg_ref,
               y_ref, xfin_ref, utm, xs, ytm, st, *, B, TB, KT, SW):
    t = pl.program_id(0)
    kw = 2 * SW
    cw = u_ref.shape[2] // KT

    @pl.when(t == 0)
    def _():
        st[...] = x0_ref[...]

    nl = u_ref.shape[2] // LANE
    for b in range(B):
        for j in range(nl):
            utm[j, pl.ds(b, TB, stride=B), :] = u_ref[b, :, j * LANE:(j + 1) * LANE]
    u_t = jnp.concatenate([utm[j] for j in range(nl)], axis=1)
    ub = u_t.astype(BF16)
    for kt in range(KT):
        xs[:, kt * kw:(kt + 1) * kw] = jnp.dot(ub[:, kt * cw:(kt + 1) * cw], bb_ref[kt],
                                               preferred_element_type=F32)

    half = SW // 2
    for kt in range(KT):
        for hf in range(2):
            c_re = kt * kw + hf * half
            c_im = c_re + SW
            l_re = lam_ref[kt:kt + 1, hf * half:(hf + 1) * half]
            l_im = lam_ref[KT + kt:KT + kt + 1, hf * half:(hf + 1) * half]

            def step(tt, carry, c_re=c_re, c_im=c_im, l_re=l_re, l_im=l_im):
                x_re, x_im = carry
                r0 = pl.multiple_of(tt * B, B)
                n_re = l_re * x_re - l_im * x_im + xs[pl.ds(r0, B), c_re:c_re + half]
                n_im = l_re * x_im + l_im * x_re + xs[pl.ds(r0, B), c_im:c_im + half]
                xs[pl.ds(r0, B), c_re:c_re + half] = n_re
                xs[pl.ds(r0, B), c_im:c_im + half] = n_im
                return n_re, n_im

            f_re, f_im = lax.fori_loop(
                0, TB, step, (st[:, c_re:c_re + half], st[:, c_im:c_im + half]), unroll=4)
            st[:, c_re:c_re + half] = f_re
            st[:, c_im:c_im + half] = f_im

    y = jnp.concatenate(
        [jnp.dot(xs[:, kt * kw:(kt + 1) * kw].astype(BF16), cc_ref[kt], preferred_element_type=F32)
         for kt in range(KT)], axis=1) + d_ref[...] * u_t
    y = 0.5 * y * (1.0 + jnp.tanh(math.sqrt(2.0 / math.pi) * (y + 0.044715 * (y * y * y))))
    y = y * _sigmoid(jnp.dot(y.astype(BF16), wg_ref[...], preferred_element_type=F32) + bg_ref[...])
    for j in range(nl):
        ytm[j] = y[:, j * LANE:(j + 1) * LANE]
    for b in range(B):
        for j in range(nl):
            y_ref[b, :, j * LANE:(j + 1) * LANE] = ytm[j, pl.ds(b, TB, stride=B), :].astype(y_ref.dtype)

    @pl.when(t == pl.num_programs(0) - 1)
    def _():
        xfin_ref[...] = st[...]


def _s5(proj3, x0, bb, cc, lam, d_row, w_glu, b_glu, TB):
    bsz, seqlen, n = proj3.shape
    width = d_row.shape[1]
    kt, cw, kw = bb.shape
    sdim = x0.shape[1]
    const = lambda shape: pl.BlockSpec(shape, lambda t: (0,) * len(shape))
    kern = functools.partial(_s5_kernel, B=bsz, TB=TB, KT=kt, SW=kw // 2)
    return pl.pallas_call(
        kern,
        grid=(seqlen // TB,),
        in_specs=[
            pl.BlockSpec((bsz, TB, width), lambda t: (0, t, n // width - 1)),
            const((bsz, sdim)), const((kt, cw, kw)), const((kt, kw, cw)), const(lam.shape),
            const((1, width)), const((width, width)), const((1, width)),
        ],
        out_specs=[pl.BlockSpec((bsz, TB, width), lambda t: (0, t, 0)), const((bsz, sdim))],
        out_shape=[jax.ShapeDtypeStruct((bsz, seqlen, width), BF16),
                   jax.ShapeDtypeStruct((bsz, sdim), F32)],
        scratch_shapes=[pltpu.VMEM((width // LANE, bsz * TB, LANE), F32),
                        pltpu.VMEM((bsz * TB, sdim), F32),
                        pltpu.VMEM((width // LANE, bsz * TB, LANE), F32),
                        pltpu.VMEM((bsz, sdim), F32)],
        compiler_params=pltpu.CompilerParams(
            dimension_semantics=("arbitrary",), vmem_limit_bytes=VMEM_LIMIT),
        name="s5",
    )(proj3, x0, bb, cc, lam, d_row, w_glu, b_glu)


def _out_ln_kernel(x_ref, o_ref, y5_ref, lng_ref, lnb_ref, wout_ref, g1_ref, b1_ref, x1_ref,
                   *, alpha):
    dn = o_ref.shape[1]
    h = _layer_norm(x_ref[...], lng_ref[...], lnb_ref[...])
    mix = (jnp.dot(o_ref[...], wout_ref[0:dn, :], preferred_element_type=F32)
           + jnp.dot(y5_ref[...], wout_ref[dn:, :], preferred_element_type=F32))
    x1_ref[...] = _layer_norm(alpha * h + mix, g1_ref[...], b1_ref[...])


def _out_ln(x2, o2, y52, ln_g, ln_b, w_out, g1, b1, alpha, tm):
    rows, d = x2.shape
    row = lambda w: pl.BlockSpec((tm, w), lambda i: (i, 0))
    vec = pl.BlockSpec((1, d), lambda i: (0, 0))
    return pl.pallas_call(
        functools.partial(_out_ln_kernel, alpha=alpha),
        grid=(rows // tm,),
        in_specs=[row(d), row(o2.shape[1]), row(y52.shape[1]), vec, vec,
                  pl.BlockSpec(w_out.shape, lambda i: (0, 0)), vec, vec],
        out_specs=row(d),
        out_shape=jax.ShapeDtypeStruct((rows, d), F32),
        compiler_params=pltpu.CompilerParams(
            dimension_semantics=("arbitrary",), vmem_limit_bytes=VMEM_LIMIT),
        name="out_ln",
    )(x2, o2, y52, ln_g, ln_b, w_out, g1, b1)


def _ffn_kernel(x1_ref, wg_ref, wu_ref, wd_ref, g2_ref, b2_ref, y_ref, x1b, *, alpha):
    j = pl.program_id(1)

    @pl.when(j == 0)
    def _():
        x1 = x1_ref[...]
        x1b[...] = x1.astype(BF16)
        y_ref[...] = alpha * x1

    xb = x1b[...]
    gate = jnp.dot(xb, wg_ref[...], preferred_element_type=F32)
    up = jnp.dot(xb, wu_ref[...], preferred_element_type=F32)
    hb = (_silu(gate) * up).astype(BF16)
    th = wd_ref.shape[0]
    for c in range(0, y_ref.shape[1], th):
        y_ref[:, c:c + th] += jnp.dot(hb, wd_ref[:, c:c + th], preferred_element_type=F32)

    @pl.when(j == pl.num_programs(1) - 1)
    def _():
        y_ref[...] = _layer_norm(y_ref[...], g2_ref[...], b2_ref[...])


def _ffn(x1, w_gate, w_up, w_down, g2, b2, alpha, tm, th):
    rows, d = x1.shape
    fh = w_gate.shape[1]
    vec = pl.BlockSpec((1, d), lambda i, j: (0, 0))
    return pl.pallas_call(
        functools.partial(_ffn_kernel, alpha=alpha),
        grid=(rows // tm, fh // th),
        in_specs=[
            pl.BlockSpec((tm, d), lambda i, j: (i, 0)),
            pl.BlockSpec((d, th), lambda i, j: (0, j)),
            pl.BlockSpec((d, th), lambda i, j: (0, j)),
            pl.BlockSpec((th, d), lambda i, j: (j, 0)),
            vec, vec,
        ],
        out_specs=pl.BlockSpec((tm, d), lambda i, j: (i, 0)),
        out_shape=jax.ShapeDtypeStruct((rows, d), F32),
        scratch_shapes=[pltpu.VMEM((tm, d), BF16)],
        compiler_params=pltpu.CompilerParams(
            dimension_semantics=("arbitrary", "arbitrary"), vmem_limit_bytes=VMEM_LIMIT),
        name="ffn",
    )(x1, w_gate, w_up, w_down, g2, b2)


def _largest_divisor(n, cap, mult):
    best = mult
    for c in range(mult, cap + 1, mult):
        if n % c == 0:
            best = c
    return best


def _layer(x3, conv_prev8, s0, x0, prm, chunk, chunks_per_step, heads_per_pack, s5_tb):
    bsz, seqlen, d = x3.shape
    rows = bsz * seqlen
    x2 = x3.reshape(rows, d)
    tm = _largest_divisor(rows, 1024, 16)
    proj, ab = _ln_inproj(x2, prm["ln_in_g"], prm["ln_in_b"], prm["w_main"], prm["w_ab"], tm,
                          _largest_divisor(prm["w_main"].shape[1], 1024, LANE))
    proj3 = proj.reshape(bsz, seqlen, -1)
    o, conv_new, s_new = _delta(proj3, ab.reshape(bsz, seqlen, LANE), conv_prev8, s0,
                                prm["cw8"], prm["gate8"], prm["nw"], chunk, chunks_per_step,
                                heads_per_pack)
    y5, x_new = _s5(proj3, x0, prm["bb"], prm["cc"], prm["lam"], prm["s5_d"], prm["w_glu"],
                    prm["b_glu"], s5_tb)
    x1 = _out_ln(x2, o.reshape(rows, -1), y5.reshape(rows, -1), prm["ln_in_g"], prm["ln_in_b"],
                 prm["w_out"], prm["ln1_g"], prm["ln1_b"], prm["alpha"],
                 _largest_divisor(rows, 512, 16))
    y = _ffn(x1, prm["w_gate"], prm["w_up"], prm["w_down"], prm["ln2_g"], prm["ln2_b"],
             prm["alpha"], tm, _largest_divisor(prm["w_gate"].shape[1], 256, LANE))
    return y.reshape(bsz, seqlen, d), conv_new, s_new, x_new


def kernel(x_prompt, x_sample, state_conv_qkv, state_delta, state_s5_re, state_s5_im, meta_tokens, ln_in_g, ln_in_b, w_in, conv_w, dn_a_log, dn_dt_bias, dn_norm_w, s5_a_re, s5_a_im, s5_log_dt, s5_b_re, s5_b_im, s5_c_re, s5_c_im, s5_d, s5_w_glu, s5_b_glu, w_out, ln1_g, ln1_b, ffn_w_gate, ffn_w_up, ffn_w_down, ln2_g, ln2_b):
    depth = w_in.shape[0]
    assert depth == 1, "single-layer trunk only"
    bp, seq, d = x_prompt.shape
    bs, dec_seq, _ = x_sample.shape
    n_meta = meta_tokens.shape[0]
    qkv_dim = conv_w.shape[-1]
    dn = qkv_dim // 3
    heads = dn // HEAD_DIM
    groups, sdim = s5_a_re.shape[1:]
    width = s5_d.shape[-1]
    assert n_meta == dec_seq and seq % CHUNK == 0 and width == groups * S5_GROUP_CH
    alpha = (2.0 * depth) ** 0.25
    row = lambda v: v.reshape(1, -1).astype(F32)

    w = w_in[0]
    n_gate = 2 * heads
    w_main = jnp.concatenate([w[:, :qkv_dim + dn], w[:, qkv_dim + dn + n_gate:]], axis=1).astype(BF16)
    w_ab = jnp.pad(w[:, qkv_dim + dn:qkv_dim + dn + n_gate], ((0, 0), (0, LANE - n_gate))).astype(BF16)
    gate8 = jnp.zeros((SUBLANE, LANE), F32)
    gate8 = gate8.at[0, :heads].set(dn_a_log[0]).at[1, :heads].set(dn_dt_bias[0])
    cw8 = jnp.pad(conv_w[0].astype(F32), ((0, SUBLANE - CONV_W), (0, 0)))

    lam_re, lam_im, f_re, f_im = _s5_disc(s5_a_re[0], s5_a_im[0], s5_log_dt[0])
    bb_re = f_re[..., None] * s5_b_re[0] - f_im[..., None] * s5_b_im[0]
    bb_im = f_re[..., None] * s5_b_im[0] + f_im[..., None] * s5_b_re[0]
    gk = GROUPS_PER_KTILE
    kt = groups // gk
    eye = jnp.eye(gk, dtype=F32)
    bb5 = jnp.stack([bb_re, bb_im]).reshape(2, kt, gk, sdim, S5_GROUP_CH)
    bb = jnp.einsum("rkgph,gG->kGhrgp", bb5, eye).reshape(kt, gk * S5_GROUP_CH, 2 * gk * sdim).astype(BF16)
    cc5 = jnp.stack([s5_c_re[0], -s5_c_im[0]]).reshape(2, kt, gk, S5_GROUP_CH, sdim)
    cc = jnp.einsum("rkghp,gG->krgpGh", cc5, eye).reshape(kt, 2 * gk * sdim, gk * S5_GROUP_CH).astype(BF16)
    lam = jnp.concatenate([lam_re.reshape(kt, gk * sdim), lam_im.reshape(kt, gk * sdim)], axis=0)

    prm = dict(
        ln_in_g=row(ln_in_g), ln_in_b=row(ln_in_b), w_main=w_main, w_ab=w_ab, cw8=cw8, gate8=gate8,
        nw=row(dn_norm_w[0]), bb=bb, cc=cc, lam=lam, s5_d=row(s5_d[0]), w_glu=s5_w_glu[0].astype(BF16),
        b_glu=row(s5_b_glu[0]), w_out=w_out[0].astype(BF16), ln1_g=row(ln1_g[0]), ln1_b=row(ln1_b[0]),
        w_gate=ffn_w_gate[0].astype(BF16), w_up=ffn_w_up[0].astype(BF16),
        w_down=ffn_w_down[0].astype(BF16), ln2_g=row(ln2_g[0]), ln2_b=row(ln2_b[0]), alpha=alpha)

    def s5_pack(re, im):
        b = re.shape[0]
        return jnp.stack([re.reshape(b, kt, gk, sdim), im.reshape(b, kt, gk, sdim)], axis=2).reshape(b, -1)

    def s5_unpack(x):
        b = x.shape[0]
        x = x.reshape(b, kt, 2, gk, sdim)
        return x[:, :, 0].reshape(b, groups, sdim), x[:, :, 1].reshape(b, groups, sdim)

    nb = 2 * SUBLANE
    assert bs + 1 <= nb
    pad_b = lambda a: jnp.pad(a.astype(F32), ((0, nb - a.shape[0]),) + ((0, 0),) * (a.ndim - 1))
    x_small = pad_b(jnp.concatenate([x_sample, meta_tokens[None].astype(x_sample.dtype)], axis=0))
    conv_small = pad_b(jnp.pad(state_conv_qkv[0], ((0, 0), (SUBLANE - CONV_W + 1, 0), (0, 0))))
    s0_small = pad_b(state_delta[0].reshape(bs, dn, HEAD_DIM))
    x0_small = pad_b(s5_pack(state_s5_re[0], state_s5_im[0]))
    y_s, conv_s, s_s, xf_s = _layer(x_small, conv_small, s0_small, x0_small, prm,
                                    dec_seq, 1, heads, dec_seq)

    bc = lambda a: jnp.broadcast_to(a[bs:bs + 1], (bp,) + a.shape[1:])
    y_p, conv_p, s_p, xf_p = _layer(x_prompt, bc(conv_s), bc(s_s), bc(xf_s), prm,
                                    CHUNK, DELTA_CHUNKS_PER_STEP, 256 // CHUNK, CHUNK)

    re_p, im_p = s5_unpack(xf_p)
    re_s, im_s = s5_unpack(xf_s[:bs])
    tail = SUBLANE - CONV_W + 1
    return (y_p.astype(x_prompt.dtype), y_s[:bs].astype(x_sample.dtype),
            conv_p[None, :, tail:].astype(state_conv_qkv.dtype),
            s_p.reshape(bp, heads, HEAD_DIM, HEAD_DIM)[None].astype(state_delta.dtype),
            re_p[None].astype(state_s5_re.dtype), im_p[None].astype(state_s5_im.dtype),
            conv_s[None, :bs, tail:].astype(state_conv_qkv.dtype),
            s_s[:bs].reshape(bs, heads, HEAD_DIM, HEAD_DIM)[None].astype(state_delta.dtype),
            re_s[None].astype(state_s5_re.dtype), im_s[None].astype(state_s5_im.dtype))
```

```python
import functools
import math

import jax
import jax.numpy as jnp
import numpy as np
from jax import lax
from jax.experimental import pallas as pl
from jax.experimental.pallas import tpu as pltpu

LN_EPS = 1e-5
RMS_EPS = 1e-6
CHUNK = 64
HEAD_DIM = 128
DELTA_CHUNKS_PER_STEP = 2
CONV_W = 4
S5_GROUP_CH = 16
GROUPS_PER_KTILE = 16
LANE = 128
SUBLANE = 8
VMEM_LIMIT = 56 * 1024 * 1024

F32 = jnp.float32
BF16 = jnp.bfloat16


def _layer_norm(x, g, b):
    mu = jnp.mean(x, axis=-1, keepdims=True)
    xc = x - mu
    var = jnp.mean(xc * xc, axis=-1, keepdims=True)
    return xc * lax.rsqrt(var + LN_EPS) * g + b


def _sigmoid(x):
    return 1.0 / (1.0 + jnp.exp(-x))


def _silu(x):
    return x * _sigmoid(x)


def _softplus(x):
    return jnp.maximum(x, 0.0) + jnp.log1p(jnp.exp(-jnp.abs(x)))


def _ln_inproj_kernel(x_ref, g_ref, b_ref, w_ref, wab_ref, proj_ref, ab_ref, h_scr):
    @pl.when(pl.program_id(1) == 0)
    def _():
        h = _layer_norm(x_ref[...], g_ref[...], b_ref[...]).astype(BF16)
        h_scr[...] = h
        ab_ref[...] = jnp.dot(h, wab_ref[...], preferred_element_type=F32)

    proj_ref[...] = jnp.dot(h_scr[...], w_ref[...], preferred_element_type=F32)


def _ln_inproj(x2, ln_g, ln_b, w_main, w_ab, tm, tn):
    rows, d = x2.shape
    n = w_main.shape[1]
    return pl.pallas_call(
        _ln_inproj_kernel,
        grid=(rows // tm, n // tn),
        in_specs=[
            pl.BlockSpec((tm, d), lambda i, j: (i, 0)),
            pl.BlockSpec((1, d), lambda i, j: (0, 0)),
            pl.BlockSpec((1, d), lambda i, j: (0, 0)),
            pl.BlockSpec((d, tn), lambda i, j: (0, j)),
            pl.BlockSpec((d, LANE), lambda i, j: (0, 0)),
        ],
        out_specs=[
            pl.BlockSpec((tm, tn), lambda i, j: (i, j)),
            pl.BlockSpec((tm, LANE), lambda i, j: (i, 0)),
        ],
        out_shape=[jax.ShapeDtypeStruct((rows, n), F32), jax.ShapeDtypeStruct((rows, LANE), F32)],
        scratch_shapes=[pltpu.VMEM((tm, d), BF16)],
        compiler_params=pltpu.CompilerParams(
            dimension_semantics=("arbitrary", "arbitrary"), vmem_limit_bytes=VMEM_LIMIT),
        name="ln_inproj",
    )(x2, ln_g, ln_b, w_main, w_ab)


def _delta_masks(C, HP):
    R, SR = HP * C, HP * HEAD_DIM
    ri = np.arange(R)[:, None]
    ci = np.arange(R)[None, :]
    same = (ri // C) == (ci // C)
    rr = [same & (ri >= ci)]
    s = 2
    while s < C:
        rr.append(((ri // (2 * s)) == (ci // (2 * s))) & ((ri // s) % 2 == 1) & ((ci // s) % 2 == 0))
        s *= 2
    first = ((ri // 2) == (ci // 2)) & (ri % 2 == 1) & (ci % 2 == 0)
    bd = (np.arange(2 * R)[:, None] % R) // C == (np.arange(SR)[None, :] // HEAD_DIM)
    return (jnp.asarray(np.stack(rr), BF16), jnp.asarray(np.stack([ri == ci, first]), F32),
            jnp.asarray(bd, BF16))


def _delta_kernel(qkv_ref, z_ref, ab_ref, cprev_ref, s0_ref, cw_ref, gate_ref, nw_ref,
                  mrr_ref, mf_ref, mbd_ref,
                  o_ref, cnew_ref, sfin_ref, cbuf, s_scr, *, C, NCH, HP, H):
    t = pl.program_id(1)
    dn = H * HEAD_DIM
    R = HP * C
    SR = HP * HEAD_DIM
    TB = NCH * C
    tail = SUBLANE

    @pl.when(t == 0)
    def _():
        cbuf[0:tail, :] = cprev_ref[0]
        s_scr[...] = s0_ref[0]

    cbuf[tail:tail + TB, :] = qkv_ref[0]

    def conv_cols(r0, c0):
        acc = cbuf[r0 + tail - 3:r0 + tail - 3 + C, c0:c0 + HEAD_DIM] * cw_ref[0:1, c0:c0 + HEAD_DIM]
        for i in range(1, CONV_W):
            acc = acc + (cbuf[r0 + tail - 3 + i:r0 + tail - 3 + i + C, c0:c0 + HEAD_DIM]
                         * cw_ref[i:i + 1, c0:c0 + HEAD_DIM])
        return _silu(acc)

    def l2n(x):
        return x * lax.rsqrt(jnp.sum(x * x, axis=-1, keepdims=True) + RMS_EPS)

    def block_diag(x, mask):
        return jnp.concatenate([x] * HP, axis=1) * mask

    bf = lambda x: x.astype(BF16)
    mm = lambda a, b: jnp.dot(a, b, preferred_element_type=F32)
    rowc = lax.broadcasted_iota(jnp.int32, (C, LANE), 0)
    reps = LANE // C
    lane_rep = lax.broadcasted_iota(jnp.int32, (1, LANE), 1) // C

    chains = [(c, p) for c in range(NCH) for p in range(H // HP)]
    gates = {}
    for c in range(NCH):
        r0 = c * C
        ab = ab_ref[0, r0:r0 + C, :]
        g_all = -jnp.exp(gate_ref[0:1, :]) * _softplus(ab + gate_ref[1:2, :])
        beta_all = _sigmoid(ab)
        gcum_all = g_all
        s = 1
        while s < C:
            gcum_all = gcum_all + jnp.where(rowc >= s, pltpu.roll(gcum_all, s, 0), 0.0)
            s *= 2
        gates[c] = (beta_all, gcum_all, gcum_all[C - 1:C, :],
                    jnp.concatenate([gcum_all] * reps, axis=0).T)

    st = {}
    for c, p in chains:
        r0 = c * C
        beta_all, gcum_all, glast_all, g_t = gates[c]
        heads = range(p * HP, (p + 1) * HP)
        qs = jnp.concatenate(
            [l2n(conv_cols(r0, h * HEAD_DIM)) * (HEAD_DIM ** -0.5) for h in heads], axis=0)
        ks = jnp.concatenate([l2n(conv_cols(r0, dn + h * HEAD_DIM)) for h in heads], axis=0)
        vs = jnp.concatenate([conv_cols(r0, 2 * dn + h * HEAD_DIM) for h in heads], axis=0)
        beta_c = jnp.concatenate([beta_all[:, H + h:H + h + 1] for h in heads], axis=0)
        gc_c = jnp.concatenate([gcum_all[:, h:h + 1] for h in heads], axis=0)
        gl_c = jnp.concatenate(
            [jnp.broadcast_to(glast_all[:, h:h + 1], (C, 1)) for h in heads], axis=0)
        row_blocks = []
        for m in range(R // LANE):
            blk = g_t[p * HP + m * reps:p * HP + m * reps + 1, :]
            for r in range(1, reps):
                hr = p * HP + m * reps + r
                blk = jnp.where(lane_rep == r, g_t[hr:hr + 1, :], blk)
            row_blocks.append(blk)
        gc_r = jnp.concatenate(row_blocks, axis=1)
        e = jnp.exp(jnp.minimum(gc_c - gc_r, 0.0))
        kb = ks * beta_c
        gexp = jnp.exp(gc_c)
        g_tot = jnp.concatenate(
            [jnp.broadcast_to(jnp.exp(glast_all[:, h:h + 1]), (HEAD_DIM, 1)) for h in heads], axis=0)
        st[c, p] = dict(
            e=e, lhs=bf(jnp.concatenate([kb, qs], axis=0)), ks_b=bf(ks),
            rhs=bf(jnp.concatenate([vs * beta_c, kb * gexp], axis=1)), q_dec=qs * gexp,
            kdec_bd=block_diag(bf(ks * jnp.exp(gl_c - gc_c)), mbd_ref[0:R, :]), g_tot=g_tot)

    for ch in chains:
        d = st[ch]
        d["kkqk"] = lax.dot_general(d["lhs"], d["ks_b"], (((1,), (1,)), ((), ())),
                                    preferred_element_type=F32)
    for ch in chains:
        d = st[ch]
        a_f = d["kkqk"][:R] * d["e"]
        d["a_b"] = bf(a_f)
        d["qk_b"] = bf(d["kkqk"][R:] * d["e"]) * mrr_ref[0]
        d["x"] = mf_ref[0] - a_f * mf_ref[1]
        d["x_b"] = bf(d["x"])
    for lvl in range(1, mrr_ref.shape[0]):
        for ch in chains:
            d = st[ch]
            d["p"] = mm(d["x_b"], d["a_b"] * mrr_ref[lvl])
        for ch in chains:
            d = st[ch]
            d["q"] = mm(bf(d["p"]), d["x_b"])
        for ch in chains:
            d = st[ch]
            d["x"] = d["x"] - d["q"]
            d["x_b"] = bf(d["x"])
    for ch in chains:
        d = st[ch]
        d["sol"] = mm(d["x_b"], d["rhs"])
    for ch in chains:
        d = st[ch]
        d["kq_bd"] = block_diag(
            bf(jnp.concatenate([d["sol"][:, HEAD_DIM:], d["q_dec"]], axis=0)), mbd_ref[...])

    packs = range(H // HP)
    for c in range(NCH):
        r0 = c * C
        s_old, ks_qs, w_b, o = {}, {}, {}, {}
        for p in packs:
            s_old[p] = s_scr[p * SR:(p + 1) * SR, :]
            ks_qs[p] = mm(st[c, p]["kq_bd"], bf(s_old[p]))
        for p in packs:
            w_b[p] = bf(st[c, p]["sol"][:, :HEAD_DIM] - ks_qs[p][:R])
            o[p] = mm(st[c, p]["qk_b"], w_b[p])
        for p in packs:
            s_scr[p * SR:(p + 1) * SR, :] = s_old[p] * st[c, p]["g_tot"] + lax.dot_general(
                st[c, p]["kdec_bd"], w_b[p], (((0,), (0,)), ((), ())), preferred_element_type=F32)
        for p in packs:
            o_p = ks_qs[p][R:] + o[p]
            for i in range(HP):
                h = p * HP + i
                oh = o_p[i * C:(i + 1) * C, :]
                oh = oh * lax.rsqrt(jnp.mean(oh * oh, axis=-1, keepdims=True) + RMS_EPS) * nw_ref[...]
                zz = z_ref[0, r0:r0 + C, h * HEAD_DIM:(h + 1) * HEAD_DIM]
                o_ref[0, r0:r0 + C, h * HEAD_DIM:(h + 1) * HEAD_DIM] = (oh * _silu(zz)).astype(o_ref.dtype)

    cbuf[0:tail, :] = cbuf[TB:TB + tail, :]
    cnew_ref[0] = cbuf[0:tail, :]

    @pl.when(t == pl.num_programs(1) - 1)
    def _():
        sfin_ref[0] = s_scr[...]


def _delta(proj3, ab3, conv_prev, s0, cw8, gate8, nw, C, NCH, HP):
    bsz, seqlen, _ = proj3.shape
    qkv_dim = cw8.shape[1]
    dn = qkv_dim // 3
    heads = dn // HEAD_DIM
    tb = NCH * C
    kern = functools.partial(_delta_kernel, C=C, NCH=NCH, HP=HP, H=heads)
    masks = _delta_masks(C, HP)
    const = lambda a: pl.BlockSpec(a.shape, lambda b, t: (0,) * a.ndim)
    return pl.pallas_call(
        kern,
        grid=(bsz, seqlen // tb),
        in_specs=[
            pl.BlockSpec((1, tb, qkv_dim), lambda b, t: (b, t, 0)),
            pl.BlockSpec((1, tb, dn), lambda b, t: (b, t, qkv_dim // dn)),
            pl.BlockSpec((1, tb, LANE), lambda b, t: (b, t, 0)),
            pl.BlockSpec((1, SUBLANE, qkv_dim), lambda b, t: (b, 0, 0)),
            pl.BlockSpec((1, dn, HEAD_DIM), lambda b, t: (b, 0, 0)),
            pl.BlockSpec((SUBLANE, qkv_dim), lambda b, t: (0, 0)),
            pl.BlockSpec((SUBLANE, LANE), lambda b, t: (0, 0)),
            pl.BlockSpec((1, HEAD_DIM), lambda b, t: (0, 0)),
            const(masks[0]), const(masks[1]), const(masks[2]),
        ],
        out_specs=[
            pl.BlockSpec((1, tb, dn), lambda b, t: (b, t, 0)),
            pl.BlockSpec((1, SUBLANE, qkv_dim), lambda b, t: (b, 0, 0)),
            pl.BlockSpec((1, dn, HEAD_DIM), lambda b, t: (b, 0, 0)),
        ],
        out_shape=[
            jax.ShapeDtypeStruct((bsz, seqlen, dn), BF16),
            jax.ShapeDtypeStruct((bsz, SUBLANE, qkv_dim), F32),
            jax.ShapeDtypeStruct((bsz, dn, HEAD_DIM), F32),
        ],
        scratch_shapes=[pltpu.VMEM((SUBLANE + tb, qkv_dim), F32), pltpu.VMEM((dn, HEAD_DIM), F32)],
        compiler_params=pltpu.CompilerParams(
            dimension_semantics=("arbitrary", "arbitrary"), vmem_limit_bytes=VMEM_LIMIT),
        name="delta_rule",
    )(proj3, proj3, ab3, conv_prev, s0, cw8, gate8, nw, *masks)


def _s5_disc_kernel(are_ref, aim_ref, logdt_ref, lre_ref, lim_ref, fre_ref, fim_ref):
    a_re = are_ref[...]
    a_im = aim_ref[...]
    dt = jnp.exp(logdt_ref[...])
    mag = jnp.exp(a_re * dt)
    lam_re = mag * jnp.cos(a_im * dt)
    lam_im = mag * jnp.sin(a_im * dt)
    den = a_re * a_re + a_im * a_im
    nr = lam_re - 1.0
    lre_ref[...] = lam_re
    lim_ref[...] = lam_im
    fre_ref[...] = (nr * a_re + lam_im * a_im) / den
    fim_ref[...] = (lam_im * a_re - nr * a_im) / den


def _s5_disc(a_re, a_im, log_dt):
    shp = jax.ShapeDtypeStruct(a_re.shape, F32)
    return pl.pallas_call(_s5_disc_kernel, out_shape=[shp] * 4, name="s5_disc")(
        a_re, a_im, log_dt.reshape(-1, 1))


def _s5_kernel(u_ref, x0_ref, bb_ref, cc_ref, lam_ref, d_ref, wg_ref, bg_ref,
               y_ref, xfin_ref, utm, xs, ytm, st, *, B, TB, KT, SW):
    t = pl.program_id(0)
    kw = 2 * SW
    cw = u_ref.shape[2] // KT

    @pl.when(t == 0)
    def _():
        st[...] = x0_ref[...]

    nl = u_ref.shape[2] // LANE
    for b in range(B):
        for j in range(nl):
            utm[j, pl.ds(b, TB, stride=B), :] = u_ref[b, :, j * LANE:(j + 1) * LANE]
    u_t = jnp.concatenate([utm[j] for j in range(nl)], axis=1)
    ub = u_t.astype(BF16)

    def input_map(kt):
        xs[:, kt * kw:(kt + 1) * kw] = jnp.dot(ub[:, kt * cw:(kt + 1) * cw], bb_ref[kt],
                                               preferred_element_type=F32)

    def recurrence(kt):
        half = SW // 2
        for hf in range(2):
            c_re = kt * kw + hf * half
            c_im = c_re + SW
            l_re = jnp.broadcast_to(lam_ref[kt:kt + 1, hf * half:(hf + 1) * half], (B, half))
            l_im = jnp.broadcast_to(lam_ref[KT + kt:KT + kt + 1, hf * half:(hf + 1) * half], (B, half))
            x_re = st[:, c_re:c_re + half]
            x_im = st[:, c_im:c_im + half]
            for tt in range(TB):
                rows = slice(tt * B, (tt + 1) * B)
                n_re = l_re * x_re - l_im * x_im + xs[rows, c_re:c_re + half]
                n_im = l_re * x_im + l_im * x_re + xs[rows, c_im:c_im + half]
                xs[rows, c_re:c_re + half] = n_re
                xs[rows, c_im:c_im + half] = n_im
                x_re, x_im = n_re, n_im
            st[:, c_re:c_re + half] = x_re
            st[:, c_im:c_im + half] = x_im

    def output_map(kt):
        return jnp.dot(xs[:, kt * kw:(kt + 1) * kw].astype(BF16), cc_ref[kt],
                       preferred_element_type=F32)

    input_map(0)
    ys = []
    for kt in range(KT):
        if kt + 1 < KT:
            input_map(kt + 1)
        recurrence(kt)
        ys.append(output_map(kt))
    y = jnp.concatenate(ys, axis=1) + d_ref[...] * u_t
    y = 0.5 * y * (1.0 + jnp.tanh(math.sqrt(2.0 / math.pi) * (y + 0.044715 * (y * y * y))))
    y = y * _sigmoid(jnp.dot(y.astype(BF16), wg_ref[...], preferred_element_type=F32) + bg_ref[...])
    for j in range(nl):
        ytm[j] = y[:, j * LANE:(j + 1) * LANE]
    for b in range(B):
        for j in range(nl):
            y_ref[b, :, j * LANE:(j + 1) * LANE] = ytm[j, pl.ds(b, TB, stride=B), :].astype(y_ref.dtype)

    @pl.when(t == pl.num_programs(0) - 1)
    def _():
        xfin_ref[...] = st[...]


def _s5(proj3, x0, bb, cc, lam, d_row, w_glu, b_glu, TB):
    bsz, seqlen, n = proj3.shape
    width = d_row.shape[1]
    kt, cw, kw = bb.shape
    sdim = x0.shape[1]
    const = lambda shape: pl.BlockSpec(shape, lambda t: (0,) * len(shape))
    kern = functools.partial(_s5_kernel, B=bsz, TB=TB, KT=kt, SW=kw // 2)
    return pl.pallas_call(
        kern,
        grid=(seqlen // TB,),
        in_specs=[
            pl.BlockSpec((bsz, TB, width), lambda t: (0, t, n // width - 1)),
            const((bsz, sdim)), const((kt, cw, kw)), const((kt, kw, cw)), const(lam.shape),
            const((1, width)), const((width, width)), const((1, width)),
        ],
        out_specs=[pl.BlockSpec((bsz, TB, width), lambda t: (0, t, 0)), const((bsz, sdim))],
        out_shape=[jax.ShapeDtypeStruct((bsz, seqlen, width), BF16),
                   jax.ShapeDtypeStruct((bsz, sdim), F32)],
        scratch_shapes=[pltpu.VMEM((width // LANE, bsz * TB, LANE), F32),
                        pltpu.VMEM((bsz * TB, sdim), F32),
                        pltpu.VMEM((width // LANE, bsz * TB, LANE), F32),
                        pltpu.VMEM((bsz, sdim), F32)],
        compiler_params=pltpu.CompilerParams(
            dimension_semantics=("arbitrary",), vmem_limit_bytes=VMEM_LIMIT),
        name="s5",
    )(proj3, x0, bb, cc, lam, d_row, w_glu, b_glu)


def _out_ln_kernel(x_ref, o_ref, y5_ref, lng_ref, lnb_ref, wout_ref, g1_ref, b1_ref,
                   x1b_ref, res_ref, *, alpha):
    dn = o_ref.shape[1]
    h = _layer_norm(x_ref[...], lng_ref[...], lnb_ref[...])
    mix = (jnp.dot(o_ref[...], wout_ref[0:dn, :], preferred_element_type=F32)
           + jnp.dot(y5_ref[...], wout_ref[dn:, :], preferred_element_type=F32))
    x1 = _layer_norm(alpha * h + mix, g1_ref[...], b1_ref[...])
    x1b_ref[...] = x1.astype(BF16)
    res_ref[...] = alpha * x1


def _out_ln(x2, o2, y52, ln_g, ln_b, w_out, g1, b1, alpha, tm):
    rows, d = x2.shape
    row = lambda w: pl.BlockSpec((tm, w), lambda i: (i, 0))
    vec = pl.BlockSpec((1, d), lambda i: (0, 0))
    return pl.pallas_call(
        functools.partial(_out_ln_kernel, alpha=alpha),
        grid=(rows // tm,),
        in_specs=[row(d), row(o2.shape[1]), row(y52.shape[1]), vec, vec,
                  pl.BlockSpec(w_out.shape, lambda i: (0, 0)), vec, vec],
        out_specs=[row(d), row(d)],
        out_shape=[jax.ShapeDtypeStruct((rows, d), BF16), jax.ShapeDtypeStruct((rows, d), F32)],
        compiler_params=pltpu.CompilerParams(
            dimension_semantics=("arbitrary",), vmem_limit_bytes=VMEM_LIMIT),
        name="out_ln",
    )(x2, o2, y52, ln_g, ln_b, w_out, g1, b1)


def _ffn_kernel(x1b_ref, res_hbm, wg_ref, wu_ref, wd_ref, g2_ref, b2_ref, y_ref, sem):
    i = pl.program_id(0)
    j = pl.program_id(1)
    tm = y_ref.shape[0]

    def residual_copy():
        return pltpu.make_async_copy(res_hbm.at[pl.ds(pl.multiple_of(i * tm, tm), tm), :], y_ref, sem)

    @pl.when(j == 0)
    def _():
        residual_copy().start()

    xb = x1b_ref[...]
    gate = jnp.dot(xb, wg_ref[...], preferred_element_type=F32)
    up = jnp.dot(xb, wu_ref[...], preferred_element_type=F32)
    hb = (_silu(gate) * up).astype(BF16)

    @pl.when(j == 0)
    def _():
        residual_copy().wait()

    th = wd_ref.shape[0]
    for c in range(0, y_ref.shape[1], th):
        y_ref[:, c:c + th] += jnp.dot(hb, wd_ref[:, c:c + th], preferred_element_type=F32)

    @pl.when(j == pl.num_programs(1) - 1)
    def _():
        y_ref[...] = _layer_norm(y_ref[...], g2_ref[...], b2_ref[...])


def _ffn(x1b, res, w_gate, w_up, w_down, g2, b2, tm, th):
    rows, d = x1b.shape
    fh = w_gate.shape[1]
    vec = pl.BlockSpec((1, d), lambda i, j: (0, 0))
    return pl.pallas_call(
        _ffn_kernel,
        grid=(rows // tm, fh // th),
        in_specs=[
            pl.BlockSpec((tm, d), lambda i, j: (i, 0)),
            pl.BlockSpec(memory_space=pl.ANY),
            pl.BlockSpec((d, th), lambda i, j: (0, j)),
            pl.BlockSpec((d, th), lambda i, j: (0, j)),
            pl.BlockSpec((th, d), lambda i, j: (j, 0)),
            vec, vec,
        ],
        out_specs=pl.BlockSpec((tm, d), lambda i, j: (i, 0)),
        out_shape=jax.ShapeDtypeStruct((rows, d), F32),
        scratch_shapes=[pltpu.SemaphoreType.DMA(())],
        compiler_params=pltpu.CompilerParams(
            dimension_semantics=("arbitrary", "arbitrary"), vmem_limit_bytes=VMEM_LIMIT),
        name="ffn",
    )(x1b, res, w_gate, w_up, w_down, g2, b2)


def _largest_divisor(n, cap, mult):
    best = mult
    for c in range(mult, cap + 1, mult):
        if n % c == 0:
            best = c
    return best


def _layer(x3, conv_prev8, s0, x0, prm, chunk, chunks_per_step, heads_per_pack, s5_tb):
    bsz, seqlen, d = x3.shape
    rows = bsz * seqlen
    x2 = x3.reshape(rows, d)
    tm = _largest_divisor(rows, 1024, 16)
    proj, ab = _ln_inproj(x2, prm["ln_in_g"], prm["ln_in_b"], prm["w_main"], prm["w_ab"], tm,
                          _largest_divisor(prm["w_main"].shape[1], 1024, LANE))
    proj3 = proj.reshape(bsz, seqlen, -1)
    o, conv_new, s_new = _delta(proj3, ab.reshape(bsz, seqlen, LANE), conv_prev8, s0,
                                prm["cw8"], prm["gate8"], prm["nw"], chunk, chunks_per_step,
                                heads_per_pack)
    y5, x_new = _s5(proj3, x0, prm["bb"], prm["cc"], prm["lam"], prm["s5_d"], prm["w_glu"],
                    prm["b_glu"], s5_tb)
    x1b, res = _out_ln(x2, o.reshape(rows, -1), y5.reshape(rows, -1), prm["ln_in_g"], prm["ln_in_b"],
                       prm["w_out"], prm["ln1_g"], prm["ln1_b"], prm["alpha"],
                       _largest_divisor(rows, 512, 16))
    y = _ffn(x1b, res, prm["w_gate"], prm["w_up"], prm["w_down"], prm["ln2_g"], prm["ln2_b"],
             tm, _largest_divisor(prm["w_gate"].shape[1], 512, LANE))
    return y.reshape(bsz, seqlen, d), conv_new, s_new, x_new


def kernel(x_prompt, x_sample, state_conv_qkv, state_delta, state_s5_re, state_s5_im, meta_tokens, ln_in_g, ln_in_b, w_in, conv_w, dn_a_log, dn_dt_bias, dn_norm_w, s5_a_re, s5_a_im, s5_log_dt, s5_b_re, s5_b_im, s5_c_re, s5_c_im, s5_d, s5_w_glu, s5_b_glu, w_out, ln1_g, ln1_b, ffn_w_gate, ffn_w_up, ffn_w_down, ln2_g, ln2_b):
    depth = w_in.shape[0]
    assert depth == 1, "single-layer trunk only"
    bp, seq, d = x_prompt.shape
    bs, dec_seq, _ = x_sample.shape
    n_meta = meta_tokens.shape[0]
    qkv_dim = conv_w.shape[-1]
    dn = qkv_dim // 3
    heads = dn // HEAD_DIM
    groups, sdim = s5_a_re.shape[1:]
    width = s5_d.shape[-1]
    assert n_meta == dec_seq and seq % CHUNK == 0 and width == groups * S5_GROUP_CH
    alpha = (2.0 * depth) ** 0.25
    row = lambda v: v.reshape(1, -1).astype(F32)

    w = w_in[0]
    n_gate = 2 * heads
    w_main = jnp.concatenate([w[:, :qkv_dim + dn], w[:, qkv_dim + dn + n_gate:]], axis=1).astype(BF16)
    w_ab = jnp.pad(w[:, qkv_dim + dn:qkv_dim + dn + n_gate], ((0, 0), (0, LANE - n_gate))).astype(BF16)
    gate8 = jnp.zeros((SUBLANE, LANE), F32)
    gate8 = gate8.at[0, :heads].set(dn_a_log[0]).at[1, :heads].set(dn_dt_bias[0])
    cw8 = jnp.pad(conv_w[0].astype(F32), ((0, SUBLANE - CONV_W), (0, 0)))

    lam_re, lam_im, f_re, f_im = _s5_disc(s5_a_re[0], s5_a_im[0], s5_log_dt[0])
    bb_re = f_re[..., None] * s5_b_re[0] - f_im[..., None] * s5_b_im[0]
    bb_im = f_re[..., None] * s5_b_im[0] + f_im[..., None] * s5_b_re[0]
    gk = GROUPS_PER_KTILE
    kt = groups // gk
    eye = jnp.eye(gk, dtype=F32)
    bb5 = jnp.stack([bb_re, bb_im]).reshape(2, kt, gk, sdim, S5_GROUP_CH)
    bb = jnp.einsum("rkgph,gG->kGhrgp", bb5, eye).reshape(kt, gk * S5_GROUP_CH, 2 * gk * sdim).astype(BF16)
    cc5 = jnp.stack([s5_c_re[0], -s5_c_im[0]]).reshape(2, kt, gk, S5_GROUP_CH, sdim)
    cc = jnp.einsum("rkghp,gG->krgpGh", cc5, eye).reshape(kt, 2 * gk * sdim, gk * S5_GROUP_CH).astype(BF16)
    lam = jnp.concatenate([lam_re.reshape(kt, gk * sdim), lam_im.reshape(kt, gk * sdim)], axis=0)

    prm = dict(
        ln_in_g=row(ln_in_g), ln_in_b=row(ln_in_b), w_main=w_main, w_ab=w_ab, cw8=cw8, gate8=gate8,
        nw=row(dn_norm_w[0]), bb=bb, cc=cc, lam=lam, s5_d=row(s5_d[0]), w_glu=s5_w_glu[0].astype(BF16),
        b_glu=row(s5_b_glu[0]), w_out=w_out[0].astype(BF16), ln1_g=row(ln1_g[0]), ln1_b=row(ln1_b[0]),
        w_gate=ffn_w_gate[0].astype(BF16), w_up=ffn_w_up[0].astype(BF16),
        w_down=ffn_w_down[0].astype(BF16), ln2_g=row(ln2_g[0]), ln2_b=row(ln2_b[0]), alpha=alpha)

    def s5_pack(re, im):
        b = re.shape[0]
        return jnp.stack([re.reshape(b, kt, gk, sdim), im.reshape(b, kt, gk, sdim)], axis=2).reshape(b, -1)

    def s5_unpack(x):
        b = x.shape[0]
        x = x.reshape(b, kt, 2, gk, sdim)
        return x[:, :, 0].reshape(b, groups, sdim), x[:, :, 1].reshape(b, groups, sdim)

    nb = 2 * SUBLANE
    assert bs + 1 <= nb
    pad_b = lambda a: jnp.pad(a.astype(F32), ((0, nb - a.shape[0]),) + ((0, 0),) * (a.ndim - 1))
    x_small = pad_b(jnp.concatenate([x_sample, meta_tokens[None].astype(x_sample.dtype)], axis=0))
    conv_small = pad_b(jnp.pad(state_conv_qkv[0], ((0, 0), (SUBLANE - CONV_W + 1, 0), (0, 0))))
    s0_small = pad_b(state_delta[0].reshape(bs, dn, HEAD_DIM))
    x0_small = pad_b(s5_pack(state_s5_re[0], state_s5_im[0]))
    y_s, conv_s, s_s, xf_s = _layer(x_small, conv_small, s0_small, x0_small, prm,
                                    dec_seq, 1, heads, dec_seq)

    bc = lambda a: jnp.broadcast_to(a[bs:bs + 1], (bp,) + a.shape[1:])
    y_p, conv_p, s_p, xf_p = _layer(x_prompt, bc(conv_s), bc(s_s), bc(xf_s), prm,
                                    CHUNK, DELTA_CHUNKS_PER_STEP, 256 // CHUNK, CHUNK)

    re_p, im_p = s5_unpack(xf_p)
    re_s, im_s = s5_unpack(xf_s[:bs])
    tail = SUBLANE - CONV_W + 1
    return (y_p.astype(x_prompt.dtype), y_s[:bs].astype(x_sample.dtype),
            conv_p[None, :, tail:].astype(state_conv_qkv.dtype),
            s_p.reshape(bp, heads, HEAD_DIM, HEAD_DIM)[None].astype(state_delta.dtype),
            re_p[None].astype(state_s5_re.dtype), im_p[None].astype(state_s5_im.dtype),
            conv_s[None, :bs, tail:].astype(state_conv_qkv.dtype),
            s_s[:bs].reshape(bs, heads, HEAD_DIM, HEAD_DIM)[None].astype(state_delta.dtype),
            re_s[None].astype(state_s5_re.dtype), im_s[None].astype(state_s5_im.dtype))
```

```python
import functools
import math

import jax
import jax.numpy as jnp
import numpy as np
from jax import lax
from jax.experimental import pallas as pl
from jax.experimental.pallas import tpu as pltpu

LN_EPS = 1e-5
RMS_EPS = 1e-6
CHUNK = 64
HEAD_DIM = 128
DELTA_CHUNKS_PER_STEP = 4
CONV_W = 4
S5_GROUP_CH = 16
GROUPS_PER_KTILE = 16
LANE = 128
SUBLANE = 8
VMEM_LIMIT = 56 * 1024 * 1024

F32 = jnp.float32
BF16 = jnp.bfloat16


def _layer_norm(x, g, b):
    mu = jnp.mean(x, axis=-1, keepdims=True)
    xc = x - mu
    var = jnp.mean(xc * xc, axis=-1, keepdims=True)
    return xc * lax.rsqrt(var + LN_EPS) * g + b


def _sigmoid(x):
    return 1.0 / (1.0 + jnp.exp(-x))


def _silu(x):
    return x * _sigmoid(x)


def _softplus(x):
    return jnp.maximum(x, 0.0) + jnp.log1p(jnp.exp(-jnp.abs(x)))


def _ln_inproj_kernel(x_ref, g_ref, b_ref, w_ref, wab_ref, proj_ref, ab_ref, h_scr):
    @pl.when(pl.program_id(1) == 0)
    def _():
        h = _layer_norm(x_ref[...], g_ref[...], b_ref[...]).astype(BF16)
        h_scr[...] = h
        ab_ref[...] = jnp.dot(h, wab_ref[...], preferred_element_type=F32)

    proj_ref[...] = jnp.dot(h_scr[...], w_ref[...], preferred_element_type=F32)


def _ln_inproj(x2, ln_g, ln_b, w_main, w_ab, tm, tn):
    rows, d = x2.shape
    n = w_main.shape[1]
    return pl.pallas_call(
        _ln_inproj_kernel,
        grid=(rows // tm, n // tn),
        in_specs=[
            pl.BlockSpec((tm, d), lambda i, j: (i, 0)),
            pl.BlockSpec((1, d), lambda i, j: (0, 0)),
            pl.BlockSpec((1, d), lambda i, j: (0, 0)),
            pl.BlockSpec((d, tn), lambda i, j: (0, j)),
            pl.BlockSpec((d, LANE), lambda i, j: (0, 0)),
        ],
        out_specs=[
            pl.BlockSpec((tm, tn), lambda i, j: (i, j)),
            pl.BlockSpec((tm, LANE), lambda i, j: (i, 0)),
        ],
        out_shape=[jax.ShapeDtypeStruct((rows, n), F32), jax.ShapeDtypeStruct((rows, LANE), F32)],
        scratch_shapes=[pltpu.VMEM((tm, d), BF16)],
        compiler_params=pltpu.CompilerParams(
            dimension_semantics=("arbitrary", "arbitrary"), vmem_limit_bytes=VMEM_LIMIT),
        name="ln_inproj",
    )(x2, ln_g, ln_b, w_main, w_ab)


def _delta_masks(C, HP):
    R, SR = HP * C, HP * HEAD_DIM
    ri = np.arange(R)[:, None]
    ci = np.arange(R)[None, :]
    same = (ri // C) == (ci // C)
    rr = [same & (ri >= ci)]
    s = 2
    while s < C:
        rr.append(((ri // (2 * s)) == (ci // (2 * s))) & ((ri // s) % 2 == 1) & ((ci // s) % 2 == 0))
        s *= 2
    first = ((ri // 2) == (ci // 2)) & (ri % 2 == 1) & (ci % 2 == 0)
    bd = (np.arange(2 * R)[:, None] % R) // C == (np.arange(SR)[None, :] // HEAD_DIM)
    return (jnp.asarray(np.stack(rr), BF16), jnp.asarray(np.stack([ri == ci, first]), F32),
            jnp.asarray(bd, BF16))


def _delta_kernel(qkv_ref, z_ref, ab_ref, cprev_ref, s0_ref, cw_ref, gate_ref, nw_ref,
                  mrr_ref, mf_ref, mbd_ref,
                  o_ref, cnew_ref, sfin_ref, cbuf, s_scr, *, C, NCH, HP, H):
    t = pl.program_id(1)
    dn = H * HEAD_DIM
    R = HP * C
    SR = HP * HEAD_DIM
    TB = NCH * C
    tail = SUBLANE

    @pl.when(t == 0)
    def _():
        cbuf[0:tail, :] = cprev_ref[0]
        s_scr[...] = s0_ref[0]

    cbuf[tail:tail + TB, :] = qkv_ref[0]

    def conv_cols(r0, c0):
        acc = cbuf[r0 + tail - 3:r0 + tail - 3 + C, c0:c0 + HEAD_DIM] * cw_ref[0:1, c0:c0 + HEAD_DIM]
        for i in range(1, CONV_W):
            acc = acc + (cbuf[r0 + tail - 3 + i:r0 + tail - 3 + i + C, c0:c0 + HEAD_DIM]
                         * cw_ref[i:i + 1, c0:c0 + HEAD_DIM])
        return _silu(acc)

    def l2n(x):
        return x * lax.rsqrt(jnp.sum(x * x, axis=-1, keepdims=True) + RMS_EPS)

    def block_diag(x, mask):
        return jnp.concatenate([x] * HP, axis=1) * mask

    bf = lambda x: x.astype(BF16)
    mm = lambda a, b: jnp.dot(a, b, preferred_element_type=F32)
    rowc = lax.broadcasted_iota(jnp.int32, (C, LANE), 0)
    reps = LANE // C
    lane_rep = lax.broadcasted_iota(jnp.int32, (1, LANE), 1) // C

    chains = [(c, p) for c in range(NCH) for p in range(H // HP)]
    gates = {}
    for c in range(NCH):
        r0 = c * C
        ab = ab_ref[0, r0:r0 + C, :]
        g_all = -jnp.exp(gate_ref[0:1, :]) * _softplus(ab + gate_ref[1:2, :])
        beta_all = _sigmoid(ab)
        gcum_all = g_all
        s = 1
        while s < C:
            gcum_all = gcum_all + jnp.where(rowc >= s, pltpu.roll(gcum_all, s, 0), 0.0)
            s *= 2
        gates[c] = (beta_all, gcum_all, gcum_all[C - 1:C, :],
                    jnp.concatenate([gcum_all] * reps, axis=0).T)

    st = {}

    def prologue(c, p):
        r0 = c * C
        beta_all, gcum_all, glast_all, g_t = gates[c]
        heads = range(p * HP, (p + 1) * HP)
        qs = jnp.concatenate(
            [l2n(conv_cols(r0, h * HEAD_DIM)) * (HEAD_DIM ** -0.5) for h in heads], axis=0)
        ks = jnp.concatenate([l2n(conv_cols(r0, dn + h * HEAD_DIM)) for h in heads], axis=0)
        vs = jnp.concatenate([conv_cols(r0, 2 * dn + h * HEAD_DIM) for h in heads], axis=0)
        beta_c = jnp.concatenate([beta_all[:, H + h:H + h + 1] for h in heads], axis=0)
        gc_c = jnp.concatenate([gcum_all[:, h:h + 1] for h in heads], axis=0)
        gl_c = jnp.concatenate(
            [jnp.broadcast_to(glast_all[:, h:h + 1], (C, 1)) for h in heads], axis=0)
        row_blocks = []
        for m in range(R // LANE):
            blk = g_t[p * HP + m * reps:p * HP + m * reps + 1, :]
            for r in range(1, reps):
                hr = p * HP + m * reps + r
                blk = jnp.where(lane_rep == r, g_t[hr:hr + 1, :], blk)
            row_blocks.append(blk)
        gc_r = jnp.concatenate(row_blocks, axis=1)
        e = jnp.exp(jnp.minimum(gc_c - gc_r, 0.0))
        kb = ks * beta_c
        gexp = jnp.exp(gc_c)
        g_tot = jnp.concatenate(
            [jnp.broadcast_to(jnp.exp(glast_all[:, h:h + 1]), (HEAD_DIM, 1)) for h in heads], axis=0)
        st[c, p] = dict(
            e=e, lhs=bf(jnp.concatenate([kb, qs], axis=0)), ks_b=bf(ks),
            rhs=bf(jnp.concatenate([vs * beta_c, kb * gexp], axis=1)), q_dec=qs * gexp,
            kdec_bd=block_diag(bf(ks * jnp.exp(gl_c - gc_c)), mbd_ref[0:R, :]), g_tot=g_tot)

    def gram_issue(chs):
        for ch in chs:
            d = st[ch]
            d["kkqk"] = lax.dot_general(d["lhs"], d["ks_b"], (((1,), (1,)), ((), ())),
                                        preferred_element_type=F32)

    def gram_post(chs):
        for ch in chs:
            d = st[ch]
            a_f = d["kkqk"][:R] * d["e"]
            d["a_b"] = bf(a_f)
            d["qk_b"] = bf(d["kkqk"][R:] * d["e"]) * mrr_ref[0]
            d["x"] = mf_ref[0] - a_f * mf_ref[1]
            d["x_b"] = bf(d["x"])

    def level_first(chs, lvl):
        for ch in chs:
            d = st[ch]
            d["p"] = mm(d["x_b"], d["a_b"] * mrr_ref[lvl])

    def level_second(chs):
        for ch in chs:
            d = st[ch]
            d["q"] = mm(bf(d["p"]), d["x_b"])

    def level_post(chs):
        for ch in chs:
            d = st[ch]
            d["x"] = d["x"] - d["q"]
            d["x_b"] = bf(d["x"])

    def solve_issue(chs):
        for ch in chs:
            d = st[ch]
            d["sol"] = mm(d["x_b"], d["rhs"])

    def solve_post(chs):
        for ch in chs:
            d = st[ch]
            d["kq_bd"] = block_diag(
                bf(jnp.concatenate([d["sol"][:, HEAD_DIM:], d["q_dec"]], axis=0)), mbd_ref[...])

    packs = range(H // HP)
    rec = {}

    def rec_read(c):
        for p in packs:
            s_old = s_scr[p * SR:(p + 1) * SR, :]
            rec[c, p] = dict(s_old=s_old, ks_qs=mm(st[c, p]["kq_bd"], bf(s_old)))

    def rec_inner(c):
        for p in packs:
            d = rec[c, p]
            d["w_b"] = bf(st[c, p]["sol"][:, :HEAD_DIM] - d["ks_qs"][:R])
            d["o"] = mm(st[c, p]["qk_b"], d["w_b"])

    def rec_write(c):
        for p in packs:
            d = rec[c, p]
            s_scr[p * SR:(p + 1) * SR, :] = d["s_old"] * st[c, p]["g_tot"] + lax.dot_general(
                st[c, p]["kdec_bd"], d["w_b"], (((0,), (0,)), ((), ())), preferred_element_type=F32)

    def rec_out(c):
        r0 = c * C
        for p in packs:
            d = rec[c, p]
            o_p = d["ks_qs"][R:] + d["o"]
            for i in range(HP):
                h = p * HP + i
                oh = o_p[i * C:(i + 1) * C, :]
                oh = oh * lax.rsqrt(jnp.mean(oh * oh, axis=-1, keepdims=True) + RMS_EPS) * nw_ref[...]
                zz = z_ref[0, r0:r0 + C, h * HEAD_DIM:(h + 1) * HEAD_DIM]
                o_ref[0, r0:r0 + C, h * HEAD_DIM:(h + 1) * HEAD_DIM] = (oh * _silu(zz)).astype(o_ref.dtype)

    group = max(NCH // 2, 1) * len(packs)
    groups = [chains[g:g + group] for g in range(0, len(chains), group)]
    fillers = []
    for ch in groups[0]:
        prologue(*ch)
    for g, chs in enumerate(groups):
        if g + 1 < len(groups):
            fillers = [functools.partial(prologue, *ch) for ch in groups[g + 1]] + fillers
        fill = lambda: fillers.pop(0)() if fillers else None
        gram_issue(chs)
        gram_post(chs)
        fill()
        for lvl in range(1, mrr_ref.shape[0]):
            level_first(chs, lvl)
            level_second(chs)
            level_post(chs)
            fill()
        solve_issue(chs)
        solve_post(chs)
        while fillers:
            fill()
        for c in sorted({c for c, _ in chs}):
            fillers += [functools.partial(rec_read, c), functools.partial(rec_inner, c),
                        lambda c=c: (rec_write(c), rec_out(c))]
    while fillers:
        fill()

    cbuf[0:tail, :] = cbuf[TB:TB + tail, :]
    cnew_ref[0] = cbuf[0:tail, :]

    @pl.when(t == pl.num_programs(1) - 1)
    def _():
        sfin_ref[0] = s_scr[...]


def _delta(proj3, ab3, conv_prev, s0, cw8, gate8, nw, C, NCH, HP):
    bsz, seqlen, _ = proj3.shape
    qkv_dim = cw8.shape[1]
    dn = qkv_dim // 3
    heads = dn // HEAD_DIM
    tb = NCH * C
    kern = functools.partial(_delta_kernel, C=C, NCH=NCH, HP=HP, H=heads)
    masks = _delta_masks(C, HP)
    const = lambda a: pl.BlockSpec(a.shape, lambda b, t: (0,) * a.ndim)
    return pl.pallas_call(
        kern,
        grid=(bsz, seqlen // tb),
        in_specs=[
            pl.BlockSpec((1, tb, qkv_dim), lambda b, t: (b, t, 0)),
            pl.BlockSpec((1, tb, dn), lambda b, t: (b, t, qkv_dim // dn)),
            pl.BlockSpec((1, tb, LANE), lambda b, t: (b, t, 0)),
            pl.BlockSpec((1, SUBLANE, qkv_dim), lambda b, t: (b, 0, 0)),
            pl.BlockSpec((1, dn, HEAD_DIM), lambda b, t: (b, 0, 0)),
            pl.BlockSpec((SUBLANE, qkv_dim), lambda b, t: (0, 0)),
            pl.BlockSpec((SUBLANE, LANE), lambda b, t: (0, 0)),
            pl.BlockSpec((1, HEAD_DIM), lambda b, t: (0, 0)),
            const(masks[0]), const(masks[1]), const(masks[2]),
        ],
        out_specs=[
            pl.BlockSpec((1, tb, dn), lambda b, t: (b, t, 0)),
            pl.BlockSpec((1, SUBLANE, qkv_dim), lambda b, t: (b, 0, 0)),
            pl.BlockSpec((1, dn, HEAD_DIM), lambda b, t: (b, 0, 0)),
        ],
        out_shape=[
            jax.ShapeDtypeStruct((bsz, seqlen, dn), BF16),
            jax.ShapeDtypeStruct((bsz, SUBLANE, qkv_dim), F32),
            jax.ShapeDtypeStruct((bsz, dn, HEAD_DIM), F32),
        ],
        scratch_shapes=[pltpu.VMEM((SUBLANE + tb, qkv_dim), F32), pltpu.VMEM((dn, HEAD_DIM), F32)],
        compiler_params=pltpu.CompilerParams(
            dimension_semantics=("arbitrary", "arbitrary"), vmem_limit_bytes=VMEM_LIMIT),
        name="delta_rule",
    )(proj3, proj3, ab3, conv_prev, s0, cw8, gate8, nw, *masks)


def _s5_disc_kernel(are_ref, aim_ref, logdt_ref, lre_ref, lim_ref, fre_ref, fim_ref):
    a_re = are_ref[...]
    a_im = aim_ref[...]
    dt = jnp.exp(logdt_ref[...])
    mag = jnp.exp(a_re * dt)
    lam_re = mag * jnp.cos(a_im * dt)
    lam_im = mag * jnp.sin(a_im * dt)
    den = a_re * a_re + a_im * a_im
    nr = lam_re - 1.0
    lre_ref[...] = lam_re
    lim_ref[...] = lam_im
    fre_ref[...] = (nr * a_re + lam_im * a_im) / den
    fim_ref[...] = (lam_im * a_re - nr * a_im) / den


def _s5_disc(a_re, a_im, log_dt):
    shp = jax.ShapeDtypeStruct(a_re.shape, F32)
    return pl.pallas_call(_s5_disc_kernel, out_shape=[shp] * 4, name="s5_disc")(
        a_re, a_im, log_dt.reshape(-1, 1))


def _s5_kernel(u_ref, x0_ref, bb_ref, cc_ref, lam_ref, d_ref, wg_ref, bg_ref,
               y_ref, xfin_ref, utm, xs, ytm, st, *, B, TB, KT, SW):
    t = pl.program_id(0)
    kw = 2 * SW
    cw = u_ref.shape[2] // KT

    @pl.when(t == 0)
    def _():
        st[...] = x0_ref[...]

    nl = u_ref.shape[2] // LANE
    for b in range(B):
        for j in range(nl):
            utm[j, pl.ds(b, TB, stride=B), :] = u_ref[b, :, j * LANE:(j + 1) * LANE]
    u_t = jnp.concatenate([utm[j] for j in range(nl)], axis=1)
    ub = u_t.astype(BF16)

    def input_map(kt):
        xs[:, kt * kw:(kt + 1) * kw] = jnp.dot(ub[:, kt * cw:(kt + 1) * cw], bb_ref[kt],
                                               preferred_element_type=F32)

    def recurrence(kt):
        half = SW // 2
        for hf in range(2):
            c_re = kt * kw + hf * half
            c_im = c_re + SW
            l_re = jnp.broadcast_to(lam_ref[kt:kt + 1, hf * half:(hf + 1) * half], (B, half))
            l_im = jnp.broadcast_to(lam_ref[KT + kt:KT + kt + 1, hf * half:(hf + 1) * half], (B, half))
            x_re = st[:, c_re:c_re + half]
            x_im = st[:, c_im:c_im + half]
            for tt in range(TB):
                rows = slice(tt * B, (tt + 1) * B)
                n_re = l_re * x_re - l_im * x_im + xs[rows, c_re:c_re + half]
                n_im = l_re * x_im + l_im * x_re + xs[rows, c_im:c_im + half]
                xs[rows, c_re:c_re + half] = n_re
                xs[rows, c_im:c_im + half] = n_im
                x_re, x_im = n_re, n_im
            st[:, c_re:c_re + half] = x_re
            st[:, c_im:c_im + half] = x_im

    def output_map(kt):
        return jnp.dot(xs[:, kt * kw:(kt + 1) * kw].astype(BF16), cc_ref[kt],
                       preferred_element_type=F32)

    input_map(0)
    ys = []
    for kt in range(KT):
        if kt + 1 < KT:
            input_map(kt + 1)
        recurrence(kt)
        ys.append(output_map(kt))
    y = jnp.concatenate(ys, axis=1) + d_ref[...] * u_t
    y = 0.5 * y * (1.0 + jnp.tanh(math.sqrt(2.0 / math.pi) * (y + 0.044715 * (y * y * y))))
    y = y * _sigmoid(jnp.dot(y.astype(BF16), wg_ref[...], preferred_element_type=F32) + bg_ref[...])
    for j in range(nl):
        ytm[j] = y[:, j * LANE:(j + 1) * LANE]
    for b in range(B):
        for j in range(nl):
            y_ref[b, :, j * LANE:(j + 1) * LANE] = ytm[j, pl.ds(b, TB, stride=B), :].astype(y_ref.dtype)

    @pl.when(t == pl.num_programs(0) - 1)
    def _():
        xfin_ref[...] = st[...]


def _s5(proj3, x0, bb, cc, lam, d_row, w_glu, b_glu, TB):
    bsz, seqlen, n = proj3.shape
    width = d_row.shape[1]
    kt, cw, kw = bb.shape
    sdim = x0.shape[1]
    const = lambda shape: pl.BlockSpec(shape, lambda t: (0,) * len(shape))
    kern = functools.partial(_s5_kernel, B=bsz, TB=TB, KT=kt, SW=kw // 2)
    return pl.pallas_call(
        kern,
        grid=(seqlen // TB,),
        in_specs=[
            pl.BlockSpec((bsz, TB, width), lambda t: (0, t, n // width - 1)),
            const((bsz, sdim)), const((kt, cw, kw)), const((kt, kw, cw)), const(lam.shape),
            const((1, width)), const((width, width)), const((1, width)),
        ],
        out_specs=[pl.BlockSpec((bsz, TB, width), lambda t: (0, t, 0)), const((bsz, sdim))],
        out_shape=[jax.ShapeDtypeStruct((bsz, seqlen, width), BF16),
                   jax.ShapeDtypeStruct((bsz, sdim), F32)],
        scratch_shapes=[pltpu.VMEM((width // LANE, bsz * TB, LANE), F32),
                        pltpu.VMEM((bsz * TB, sdim), F32),
                        pltpu.VMEM((width // LANE, bsz * TB, LANE), F32),
                        pltpu.VMEM((bsz, sdim), F32)],
        compiler_params=pltpu.CompilerParams(
            dimension_semantics=("arbitrary",), vmem_limit_bytes=VMEM_LIMIT),
        name="s5",
    )(proj3, x0, bb, cc, lam, d_row, w_glu, b_glu)


def _out_ln_kernel(x_ref, o_ref, y5_ref, lng_ref, lnb_ref, wout_ref, g1_ref, b1_ref,
                   x1b_ref, res_ref, *, alpha):
    dn = o_ref.shape[1]
    h = _layer_norm(x_ref[...], lng_ref[...], lnb_ref[...])
    mix = (jnp.dot(o_ref[...], wout_ref[0:dn, :], preferred_element_type=F32)
           + jnp.dot(y5_ref[...], wout_ref[dn:, :], preferred_element_type=F32))
    x1 = _layer_norm(alpha * h + mix, g1_ref[...], b1_ref[...])
    x1b_ref[...] = x1.astype(BF16)
    res_ref[...] = alpha * x1


def _out_ln(x2, o2, y52, ln_g, ln_b, w_out, g1, b1, alpha, tm):
    rows, d = x2.shape
    row = lambda w: pl.BlockSpec((tm, w), lambda i: (i, 0))
    vec = pl.BlockSpec((1, d), lambda i: (0, 0))
    return pl.pallas_call(
        functools.partial(_out_ln_kernel, alpha=alpha),
        grid=(rows // tm,),
        in_specs=[row(d), row(o2.shape[1]), row(y52.shape[1]), vec, vec,
                  pl.BlockSpec(w_out.shape, lambda i: (0, 0)), vec, vec],
        out_specs=[row(d), row(d)],
        out_shape=[jax.ShapeDtypeStruct((rows, d), BF16), jax.ShapeDtypeStruct((rows, d), F32)],
        compiler_params=pltpu.CompilerParams(
            dimension_semantics=("arbitrary",), vmem_limit_bytes=VMEM_LIMIT),
        name="out_ln",
    )(x2, o2, y52, ln_g, ln_b, w_out, g1, b1)


def _ffn_kernel(x1b_ref, res_hbm, wg_ref, wu_ref, wd_ref, g2_ref, b2_ref, y_ref, sem):
    i = pl.program_id(0)
    j = pl.program_id(1)
    tm, d = y_ref.shape
    th = wd_ref.shape[0]
    col_chunks = list(enumerate(range(0, d, th)))

    def residual_copy(k, c):
        return pltpu.make_async_copy(
            res_hbm.at[pl.ds(pl.multiple_of(i * tm, tm), tm), pl.ds(c, th)],
            y_ref.at[:, pl.ds(c, th)], sem.at[k])

    @pl.when(j == 0)
    def _():
        for k, c in col_chunks:
            residual_copy(k, c).start()

    xb = x1b_ref[...]
    gate = jnp.dot(xb, wg_ref[...], preferred_element_type=F32)
    up = jnp.dot(xb, wu_ref[...], preferred_element_type=F32)
    hb = (_silu(gate) * up).astype(BF16)

    for k, c in col_chunks:
        @pl.when(j == 0)
        def _(k=k, c=c):
            residual_copy(k, c).wait()

        y_ref[:, c:c + th] += jnp.dot(hb, wd_ref[:, c:c + th], preferred_element_type=F32)

    @pl.when(j == pl.num_programs(1) - 1)
    def _():
        y_ref[...] = _layer_norm(y_ref[...], g2_ref[...], b2_ref[...])


def _ffn(x1b, res, w_gate, w_up, w_down, g2, b2, tm, th):
    rows, d = x1b.shape
    fh = w_gate.shape[1]
    vec = pl.BlockSpec((1, d), lambda i, j: (0, 0))
    return pl.pallas_call(
        _ffn_kernel,
        grid=(rows // tm, fh // th),
        in_specs=[
            pl.BlockSpec((tm, d), lambda i, j: (i, 0)),
            pl.BlockSpec(memory_space=pl.ANY),
            pl.BlockSpec((d, th), lambda i, j: (0, j)),
            pl.BlockSpec((d, th), lambda i, j: (0, j)),
            pl.BlockSpec((th, d), lambda i, j: (j, 0)),
            vec, vec,
        ],
        out_specs=pl.BlockSpec((tm, d), lambda i, j: (i, 0)),
        out_shape=jax.ShapeDtypeStruct((rows, d), F32),
        scratch_shapes=[pltpu.SemaphoreType.DMA((d // th,))],
        compiler_params=pltpu.CompilerParams(
            dimension_semantics=("arbitrary", "arbitrary"), vmem_limit_bytes=VMEM_LIMIT),
        name="ffn",
    )(x1b, res, w_gate, w_up, w_down, g2, b2)


def _largest_divisor(n, cap, mult):
    best = mult
    for c in range(mult, cap + 1, mult):
        if n % c == 0:
            best = c
    return best


def _layer(x3, conv_prev8, s0, x0, prm, chunk, chunks_per_step, heads_per_pack, s5_tb):
    bsz, seqlen, d = x3.shape
    rows = bsz * seqlen
    x2 = x3.reshape(rows, d)
    tm = _largest_divisor(rows, 1024, 16)
    proj, ab = _ln_inproj(x2, prm["ln_in_g"], prm["ln_in_b"], prm["w_main"], prm["w_ab"], tm,
                          _largest_divisor(prm["w_main"].shape[1], 1024, LANE))
    proj3 = proj.reshape(bsz, seqlen, -1)
    o, conv_new, s_new = _delta(proj3, ab.reshape(bsz, seqlen, LANE), conv_prev8, s0,
                                prm["cw8"], prm["gate8"], prm["nw"], chunk, chunks_per_step,
                                heads_per_pack)
    y5, x_new = _s5(proj3, x0, prm["bb"], prm["cc"], prm["lam"], prm["s5_d"], prm["w_glu"],
                    prm["b_glu"], s5_tb)
    x1b, res = _out_ln(x2, o.reshape(rows, -1), y5.reshape(rows, -1), prm["ln_in_g"], prm["ln_in_b"],
                       prm["w_out"], prm["ln1_g"], prm["ln1_b"], prm["alpha"],
                       _largest_divisor(rows, 512, 16))
    y = _ffn(x1b, res, prm["w_gate"], prm["w_up"], prm["w_down"], prm["ln2_g"], prm["ln2_b"],
             tm, _largest_divisor(prm["w_gate"].shape[1], 512, LANE))
    return y.reshape(bsz, seqlen, d), conv_new, s_new, x_new


def kernel(x_prompt, x_sample, state_conv_qkv, state_delta, state_s5_re, state_s5_im, meta_tokens, ln_in_g, ln_in_b, w_in, conv_w, dn_a_log, dn_dt_bias, dn_norm_w, s5_a_re, s5_a_im, s5_log_dt, s5_b_re, s5_b_im, s5_c_re, s5_c_im, s5_d, s5_w_glu, s5_b_glu, w_out, ln1_g, ln1_b, ffn_w_gate, ffn_w_up, ffn_w_down, ln2_g, ln2_b):
    depth = w_in.shape[0]
    assert depth == 1, "single-layer trunk only"
    bp, seq, d = x_prompt.shape
    bs, dec_seq, _ = x_sample.shape
    n_meta = meta_tokens.shape[0]
    qkv_dim = conv_w.shape[-1]
    dn = qkv_dim // 3
    heads = dn // HEAD_DIM
    groups, sdim = s5_a_re.shape[1:]
    width = s5_d.shape[-1]
    assert n_meta == dec_seq and seq % CHUNK == 0 and width == groups * S5_GROUP_CH
    alpha = (2.0 * depth) ** 0.25
    row = lambda v: v.reshape(1, -1).astype(F32)

    w = w_in[0]
    n_gate = 2 * heads
    w_main = jnp.concatenate([w[:, :qkv_dim + dn], w[:, qkv_dim + dn + n_gate:]], axis=1).astype(BF16)
    w_ab = jnp.pad(w[:, qkv_dim + dn:qkv_dim + dn + n_gate], ((0, 0), (0, LANE - n_gate))).astype(BF16)
    gate8 = jnp.zeros((SUBLANE, LANE), F32)
    gate8 = gate8.at[0, :heads].set(dn_a_log[0]).at[1, :heads].set(dn_dt_bias[0])
    cw8 = jnp.pad(conv_w[0].astype(F32), ((0, SUBLANE - CONV_W), (0, 0)))

    lam_re, lam_im, f_re, f_im = _s5_disc(s5_a_re[0], s5_a_im[0], s5_log_dt[0])
    bb_re = f_re[..., None] * s5_b_re[0] - f_im[..., None] * s5_b_im[0]
    bb_im = f_re[..., None] * s5_b_im[0] + f_im[..., None] * s5_b_re[0]
    gk = GROUPS_PER_KTILE
    kt = groups // gk
    eye = jnp.eye(gk, dtype=F32)
    bb5 = jnp.stack([bb_re, bb_im]).reshape(2, kt, gk, sdim, S5_GROUP_CH)
    bb = jnp.einsum("rkgph,gG->kGhrgp", bb5, eye).reshape(kt, gk * S5_GROUP_CH, 2 * gk * sdim).astype(BF16)
    cc5 = jnp.stack([s5_c_re[0], -s5_c_im[0]]).reshape(2, kt, gk, S5_GROUP_CH, sdim)
    cc = jnp.einsum("rkghp,gG->krgpGh", cc5, eye).reshape(kt, 2 * gk * sdim, gk * S5_GROUP_CH).astype(BF16)
    lam = jnp.concatenate([lam_re.reshape(kt, gk * sdim), lam_im.reshape(kt, gk * sdim)], axis=0)

    prm = dict(
        ln_in_g=row(ln_in_g), ln_in_b=row(ln_in_b), w_main=w_main, w_ab=w_ab, cw8=cw8, gate8=gate8,
        nw=row(dn_norm_w[0]), bb=bb, cc=cc, lam=lam, s5_d=row(s5_d[0]), w_glu=s5_w_glu[0].astype(BF16),
        b_glu=row(s5_b_glu[0]), w_out=w_out[0].astype(BF16), ln1_g=row(ln1_g[0]), ln1_b=row(ln1_b[0]),
        w_gate=ffn_w_gate[0].astype(BF16), w_up=ffn_w_up[0].astype(BF16),
        w_down=ffn_w_down[0].astype(BF16), ln2_g=row(ln2_g[0]), ln2_b=row(ln2_b[0]), alpha=alpha)

    def s5_pack(re, im):
        b = re.shape[0]
        return jnp.stack([re.reshape(b, kt, gk, sdim), im.reshape(b, kt, gk, sdim)], axis=2).reshape(b, -1)

    def s5_unpack(x):
        b = x.shape[0]
        x = x.reshape(b, kt, 2, gk, sdim)
        return x[:, :, 0].reshape(b, groups, sdim), x[:, :, 1].reshape(b, groups, sdim)

    nb = 2 * SUBLANE
    assert bs + 1 <= nb
    pad_b = lambda a: jnp.pad(a.astype(F32), ((0, nb - a.shape[0]),) + ((0, 0),) * (a.ndim - 1))
    x_small = pad_b(jnp.concatenate([x_sample, meta_tokens[None].astype(x_sample.dtype)], axis=0))
    conv_small = pad_b(jnp.pad(state_conv_qkv[0], ((0, 0), (SUBLANE - CONV_W + 1, 0), (0, 0))))
    s0_small = pad_b(state_delta[0].reshape(bs, dn, HEAD_DIM))
    x0_small = pad_b(s5_pack(state_s5_re[0], state_s5_im[0]))
    y_s, conv_s, s_s, xf_s = _layer(x_small, conv_small, s0_small, x0_small, prm,
                                    dec_seq, 1, heads, dec_seq)

    bc = lambda a: jnp.broadcast_to(a[bs:bs + 1], (bp,) + a.shape[1:])
    y_p, conv_p, s_p, xf_p = _layer(x_prompt, bc(conv_s), bc(s_s), bc(xf_s), prm,
                                    CHUNK, DELTA_CHUNKS_PER_STEP, 256 // CHUNK, CHUNK)

    re_p, im_p = s5_unpack(xf_p)
    re_s, im_s = s5_unpack(xf_s[:bs])
    tail = SUBLANE - CONV_W + 1
    return (y_p.astype(x_prompt.dtype), y_s[:bs].astype(x_sample.dtype),
            conv_p[None, :, tail:].astype(state_conv_qkv.dtype),
            s_p.reshape(bp, heads, HEAD_DIM, HEAD_DIM)[None].astype(state_delta.dtype),
            re_p[None].astype(state_s5_re.dtype), im_p[None].astype(state_s5_im.dtype),
            conv_s[None, :bs, tail:].astype(state_conv_qkv.dtype),
            s_s[:bs].reshape(bs, heads, HEAD_DIM, HEAD_DIM)[None].astype(state_delta.dtype),
            re_s[None].astype(state_s5_re.dtype), im_s[None].astype(state_s5_im.dtype))
```

```python
import functools
import math

import jax
import jax.numpy as jnp
import numpy as np
from jax import lax
from jax.experimental import pallas as pl
from jax.experimental.pallas import tpu as pltpu

LN_EPS = 1e-5
RMS_EPS = 1e-6
CHUNK = 64
HEAD_DIM = 128
DELTA_CHUNKS_PER_STEP = 4
CONV_W = 4
S5_GROUP_CH = 16
GROUPS_PER_KTILE = 16
LANE = 128
SUBLANE = 8
VMEM_LIMIT = 56 * 1024 * 1024
FFN_VMEM_LIMIT = 60 * 1024 * 1024

F32 = jnp.float32
BF16 = jnp.bfloat16


def _layer_norm(x, g, b):
    mu = jnp.mean(x, axis=-1, keepdims=True)
    xc = x - mu
    var = jnp.mean(xc * xc, axis=-1, keepdims=True)
    return xc * lax.rsqrt(var + LN_EPS) * g + b


def _sigmoid(x):
    return 1.0 / (1.0 + jnp.exp(-x))


def _silu(x):
    return x * _sigmoid(x)


def _softplus(x):
    return jnp.maximum(x, 0.0) + jnp.log1p(jnp.exp(-jnp.abs(x)))


def _ln_inproj_kernel(x_ref, g_ref, b_ref, w_ref, wab_ref, proj_ref, ab_ref, h_scr):
    @pl.when(pl.program_id(1) == 0)
    def _():
        h = _layer_norm(x_ref[...], g_ref[...], b_ref[...]).astype(BF16)
        h_scr[...] = h
        ab_ref[...] = jnp.dot(h, wab_ref[...], preferred_element_type=F32)

    proj_ref[...] = jnp.dot(h_scr[...], w_ref[...], preferred_element_type=F32)


def _ln_inproj(x2, ln_g, ln_b, w_main, w_ab, tm, tn):
    rows, d = x2.shape
    n = w_main.shape[1]
    return pl.pallas_call(
        _ln_inproj_kernel,
        grid=(rows // tm, n // tn),
        in_specs=[
            pl.BlockSpec((tm, d), lambda i, j: (i, 0)),
            pl.BlockSpec((1, d), lambda i, j: (0, 0)),
            pl.BlockSpec((1, d), lambda i, j: (0, 0)),
            pl.BlockSpec((d, tn), lambda i, j: (0, j)),
            pl.BlockSpec((d, LANE), lambda i, j: (0, 0)),
        ],
        out_specs=[
            pl.BlockSpec((tm, tn), lambda i, j: (i, j)),
            pl.BlockSpec((tm, LANE), lambda i, j: (i, 0)),
        ],
        out_shape=[jax.ShapeDtypeStruct((rows, n), F32), jax.ShapeDtypeStruct((rows, LANE), F32)],
        scratch_shapes=[pltpu.VMEM((tm, d), BF16)],
        compiler_params=pltpu.CompilerParams(
            dimension_semantics=("arbitrary", "arbitrary"), vmem_limit_bytes=VMEM_LIMIT),
        name="ln_inproj",
    )(x2, ln_g, ln_b, w_main, w_ab)


def _delta_masks(C, HP):
    R, SR = HP * C, HP * HEAD_DIM
    ri = np.arange(R)[:, None]
    ci = np.arange(R)[None, :]
    same = (ri // C) == (ci // C)
    rr = [same & (ri >= ci)]
    s = 2
    while s < C:
        rr.append(((ri // (2 * s)) == (ci // (2 * s))) & ((ri // s) % 2 == 1) & ((ci // s) % 2 == 0))
        s *= 2
    first = ((ri // 2) == (ci // 2)) & (ri % 2 == 1) & (ci % 2 == 0)
    bd = (np.arange(2 * R)[:, None] % R) // C == (np.arange(SR)[None, :] // HEAD_DIM)
    return (jnp.asarray(np.stack(rr), BF16), jnp.asarray(np.stack([ri == ci, first]), F32),
            jnp.asarray(bd, BF16))


def _delta_kernel(qkv_ref, z_ref, ab_ref, cprev_ref, s0_ref, cw_ref, gate_ref, nw_ref,
                  mrr_ref, mf_ref, mbd_ref,
                  o_ref, cnew_ref, sfin_ref, cbuf, s_scr, *, C, NCH, HP, H):
    t = pl.program_id(1)
    dn = H * HEAD_DIM
    R = HP * C
    SR = HP * HEAD_DIM
    TB = NCH * C
    tail = SUBLANE

    @pl.when(t == 0)
    def _():
        cbuf[0:tail, :] = cprev_ref[0]
        s_scr[...] = s0_ref[0]

    cbuf[tail:tail + TB, :] = qkv_ref[0]

    def conv_cols(r0, c0):
        acc = cbuf[r0 + tail - 3:r0 + tail - 3 + C, c0:c0 + HEAD_DIM] * cw_ref[0:1, c0:c0 + HEAD_DIM]
        for i in range(1, CONV_W):
            acc = acc + (cbuf[r0 + tail - 3 + i:r0 + tail - 3 + i + C, c0:c0 + HEAD_DIM]
                         * cw_ref[i:i + 1, c0:c0 + HEAD_DIM])
        return _silu(acc)

    def l2n(x):
        return x * lax.rsqrt(jnp.sum(x * x, axis=-1, keepdims=True) + RMS_EPS)

    def block_diag(x, mask):
        return jnp.concatenate([x] * HP, axis=1) * mask

    bf = lambda x: x.astype(BF16)
    mm = lambda a, b: jnp.dot(a, b, preferred_element_type=F32)
    rowc = lax.broadcasted_iota(jnp.int32, (C, LANE), 0)
    reps = LANE // C
    lane_rep = lax.broadcasted_iota(jnp.int32, (1, LANE), 1) // C

    chains = [(c, p) for c in range(NCH) for p in range(H // HP)]
    gates = {}
    for c in range(NCH):
        r0 = c * C
        ab = ab_ref[0, r0:r0 + C, :]
        g_all = -jnp.exp(gate_ref[0:1, :]) * _softplus(ab + gate_ref[1:2, :])
        beta_all = _sigmoid(ab)
        gcum_all = g_all
        s = 1
        while s < C:
            gcum_all = gcum_all + jnp.where(rowc >= s, pltpu.roll(gcum_all, s, 0), 0.0)
            s *= 2
        gates[c] = (beta_all, gcum_all, gcum_all[C - 1:C, :],
                    jnp.concatenate([gcum_all] * reps, axis=0).T)

    st = {}

    def prologue(c, p):
        r0 = c * C
        beta_all, gcum_all, glast_all, g_t = gates[c]
        heads = range(p * HP, (p + 1) * HP)
        qs = jnp.concatenate(
            [l2n(conv_cols(r0, h * HEAD_DIM)) * (HEAD_DIM ** -0.5) for h in heads], axis=0)
        ks = jnp.concatenate([l2n(conv_cols(r0, dn + h * HEAD_DIM)) for h in heads], axis=0)
        vs = jnp.concatenate([conv_cols(r0, 2 * dn + h * HEAD_DIM) for h in heads], axis=0)
        beta_c = jnp.concatenate([beta_all[:, H + h:H + h + 1] for h in heads], axis=0)
        gc_c = jnp.concatenate([gcum_all[:, h:h + 1] for h in heads], axis=0)
        gl_c = jnp.concatenate(
            [jnp.broadcast_to(glast_all[:, h:h + 1], (C, 1)) for h in heads], axis=0)
        row_blocks = []
        for m in range(R // LANE):
            blk = g_t[p * HP + m * reps:p * HP + m * reps + 1, :]
            for r in range(1, reps):
                hr = p * HP + m * reps + r
                blk = jnp.where(lane_rep == r, g_t[hr:hr + 1, :], blk)
            row_blocks.append(blk)
        gc_r = jnp.concatenate(row_blocks, axis=1)
        e = jnp.exp(jnp.minimum(gc_c - gc_r, 0.0))
        kb = ks * beta_c
        gexp = jnp.exp(gc_c)
        g_tot = jnp.concatenate(
            [jnp.broadcast_to(jnp.exp(glast_all[:, h:h + 1]), (HEAD_DIM, 1)) for h in heads], axis=0)
        st[c, p] = dict(
            e=e, lhs=bf(jnp.concatenate([kb, qs], axis=0)), ks_b=bf(ks),
            rhs=bf(jnp.concatenate([vs * beta_c, kb * gexp], axis=1)), q_dec=qs * gexp,
            kdec_bd=block_diag(bf(ks * jnp.exp(gl_c - gc_c)), mbd_ref[0:R, :]), g_tot=g_tot)

    def gram_issue(chs):
        for ch in chs:
            d = st[ch]
            d["kkqk"] = lax.dot_general(d["lhs"], d["ks_b"], (((1,), (1,)), ((), ())),
                                        preferred_element_type=F32)

    def gram_post(chs):
        for ch in chs:
            d = st[ch]
            a_f = d["kkqk"][:R] * d["e"]
            d["a_b"] = bf(a_f)
            d["qk_b"] = bf(d["kkqk"][R:] * d["e"]) * mrr_ref[0]
            d["x"] = mf_ref[0] - a_f * mf_ref[1]
            d["x_b"] = bf(d["x"])

    def level_first(chs, lvl):
        for ch in chs:
            d = st[ch]
            d["p"] = mm(d["x_b"], d["a_b"] * mrr_ref[lvl])

    def level_second(chs):
        for ch in chs:
            d = st[ch]
            d["q"] = mm(bf(d["p"]), d["x_b"])

    def level_post(chs):
        for ch in chs:
            d = st[ch]
            d["x"] = d["x"] - d["q"]
            d["x_b"] = bf(d["x"])

    def solve_issue(chs):
        for ch in chs:
            d = st[ch]
            d["sol"] = mm(d["x_b"], d["rhs"])

    def solve_post(chs):
        for ch in chs:
            d = st[ch]
            d["kq_bd"] = block_diag(
                bf(jnp.concatenate([d["sol"][:, HEAD_DIM:], d["q_dec"]], axis=0)), mbd_ref[...])

    packs = range(H // HP)
    rec = {}

    def rec_read(c):
        for p in packs:
            s_old = s_scr[p * SR:(p + 1) * SR, :]
            rec[c, p] = dict(s_old=s_old, ks_qs=mm(st[c, p]["kq_bd"], bf(s_old)))

    def rec_inner(c):
        for p in packs:
            d = rec[c, p]
            d["w_b"] = bf(st[c, p]["sol"][:, :HEAD_DIM] - d["ks_qs"][:R])
            d["o"] = mm(st[c, p]["qk_b"], d["w_b"])

    def rec_write(c):
        for p in packs:
            d = rec[c, p]
            s_scr[p * SR:(p + 1) * SR, :] = d["s_old"] * st[c, p]["g_tot"] + lax.dot_general(
                st[c, p]["kdec_bd"], d["w_b"], (((0,), (0,)), ((), ())), preferred_element_type=F32)

    def rec_out(c):
        r0 = c * C
        for p in packs:
            d = rec[c, p]
            o_p = d["ks_qs"][R:] + d["o"]
            for i in range(HP):
                h = p * HP + i
                oh = o_p[i * C:(i + 1) * C, :]
                oh = oh * lax.rsqrt(jnp.mean(oh * oh, axis=-1, keepdims=True) + RMS_EPS) * nw_ref[...]
                zz = z_ref[0, r0:r0 + C, h * HEAD_DIM:(h + 1) * HEAD_DIM]
                o_ref[0, r0:r0 + C, h * HEAD_DIM:(h + 1) * HEAD_DIM] = (oh * _silu(zz)).astype(o_ref.dtype)

    group = max(NCH // 2, 1) * len(packs)
    groups = [chains[g:g + group] for g in range(0, len(chains), group)]
    fillers = []
    for ch in groups[0]:
        prologue(*ch)
    for g, chs in enumerate(groups):
        if g + 1 < len(groups):
            fillers = [functools.partial(prologue, *ch) for ch in groups[g + 1]] + fillers
        fill = lambda: fillers.pop(0)() if fillers else None
        gram_issue(chs)
        gram_post(chs)
        fill()
        for lvl in range(1, mrr_ref.shape[0]):
            level_first(chs, lvl)
            level_second(chs)
            level_post(chs)
            fill()
        solve_issue(chs)
        solve_post(chs)
        while fillers:
            fill()
        for c in sorted({c for c, _ in chs}):
            fillers += [functools.partial(rec_read, c), functools.partial(rec_inner, c),
                        lambda c=c: (rec_write(c), rec_out(c))]
    while fillers:
        fill()

    cbuf[0:tail, :] = cbuf[TB:TB + tail, :]
    cnew_ref[0] = cbuf[0:tail, :]

    @pl.when(t == pl.num_programs(1) - 1)
    def _():
        sfin_ref[0] = s_scr[...]


def _delta(proj3, ab3, conv_prev, s0, cw8, gate8, nw, C, NCH, HP):
    bsz, seqlen, _ = proj3.shape
    qkv_dim = cw8.shape[1]
    dn = qkv_dim // 3
    heads = dn // HEAD_DIM
    tb = NCH * C
    kern = functools.partial(_delta_kernel, C=C, NCH=NCH, HP=HP, H=heads)
    masks = _delta_masks(C, HP)
    const = lambda a: pl.BlockSpec(a.shape, lambda b, t: (0,) * a.ndim)
    return pl.pallas_call(
        kern,
        grid=(bsz, seqlen // tb),
        in_specs=[
            pl.BlockSpec((1, tb, qkv_dim), lambda b, t: (b, t, 0)),
            pl.BlockSpec((1, tb, dn), lambda b, t: (b, t, qkv_dim // dn)),
            pl.BlockSpec((1, tb, LANE), lambda b, t: (b, t, 0)),
            pl.BlockSpec((1, SUBLANE, qkv_dim), lambda b, t: (b, 0, 0)),
            pl.BlockSpec((1, dn, HEAD_DIM), lambda b, t: (b, 0, 0)),
            pl.BlockSpec((SUBLANE, qkv_dim), lambda b, t: (0, 0)),
            pl.BlockSpec((SUBLANE, LANE), lambda b, t: (0, 0)),
            pl.BlockSpec((1, HEAD_DIM), lambda b, t: (0, 0)),
            const(masks[0]), const(masks[1]), const(masks[2]),
        ],
        out_specs=[
            pl.BlockSpec((1, tb, dn), lambda b, t: (b, t, 0)),
            pl.BlockSpec((1, SUBLANE, qkv_dim), lambda b, t: (b, 0, 0)),
            pl.BlockSpec((1, dn, HEAD_DIM), lambda b, t: (b, 0, 0)),
        ],
        out_shape=[
            jax.ShapeDtypeStruct((bsz, seqlen, dn), BF16),
            jax.ShapeDtypeStruct((bsz, SUBLANE, qkv_dim), F32),
            jax.ShapeDtypeStruct((bsz, dn, HEAD_DIM), F32),
        ],
        scratch_shapes=[pltpu.VMEM((SUBLANE + tb, qkv_dim), F32), pltpu.VMEM((dn, HEAD_DIM), F32)],
        compiler_params=pltpu.CompilerParams(
            dimension_semantics=("arbitrary", "arbitrary"), vmem_limit_bytes=VMEM_LIMIT),
        name="delta_rule",
    )(proj3, proj3, ab3, conv_prev, s0, cw8, gate8, nw, *masks)


def _s5_disc_kernel(are_ref, aim_ref, logdt_ref, lre_ref, lim_ref, fre_ref, fim_ref):
    a_re = are_ref[...]
    a_im = aim_ref[...]
    dt = jnp.exp(logdt_ref[...])
    mag = jnp.exp(a_re * dt)
    lam_re = mag * jnp.cos(a_im * dt)
    lam_im = mag * jnp.sin(a_im * dt)
    den = a_re * a_re + a_im * a_im
    nr = lam_re - 1.0
    lre_ref[...] = lam_re
    lim_ref[...] = lam_im
    fre_ref[...] = (nr * a_re + lam_im * a_im) / den
    fim_ref[...] = (lam_im * a_re - nr * a_im) / den


def _s5_disc(a_re, a_im, log_dt):
    shp = jax.ShapeDtypeStruct(a_re.shape, F32)
    return pl.pallas_call(_s5_disc_kernel, out_shape=[shp] * 4, name="s5_disc")(
        a_re, a_im, log_dt.reshape(-1, 1))


def _s5_kernel(u_ref, x0_ref, bb_ref, cc_ref, lam_ref, d_ref, wg_ref, bg_ref,
               y_ref, xfin_ref, utm, xs, ytm, st, *, B, TB, KT, SW):
    t = pl.program_id(0)
    kw = 2 * SW
    cw = u_ref.shape[2] // KT

    @pl.when(t == 0)
    def _():
        st[...] = x0_ref[...]

    nl = u_ref.shape[2] // LANE
    for b in range(B):
        for j in range(nl):
            utm[j, pl.ds(b, TB, stride=B), :] = u_ref[b, :, j * LANE:(j + 1) * LANE]
    u_t = jnp.concatenate([utm[j] for j in range(nl)], axis=1)
    ub = u_t.astype(BF16)

    def input_map(kt):
        xs[:, kt * kw:(kt + 1) * kw] = jnp.dot(ub[:, kt * cw:(kt + 1) * cw], bb_ref[kt],
                                               preferred_element_type=F32)

    def recurrence(kt):
        half = SW // 2
        for hf in range(2):
            c_re = kt * kw + hf * half
            c_im = c_re + SW
            l_re = jnp.broadcast_to(lam_ref[kt:kt + 1, hf * half:(hf + 1) * half], (B, half))
            l_im = jnp.broadcast_to(lam_ref[KT + kt:KT + kt + 1, hf * half:(hf + 1) * half], (B, half))
            x_re = st[:, c_re:c_re + half]
            x_im = st[:, c_im:c_im + half]
            for tt in range(TB):
                rows = slice(tt * B, (tt + 1) * B)
                n_re = l_re * x_re - l_im * x_im + xs[rows, c_re:c_re + half]
                n_im = l_re * x_im + l_im * x_re + xs[rows, c_im:c_im + half]
                xs[rows, c_re:c_re + half] = n_re
                xs[rows, c_im:c_im + half] = n_im
                x_re, x_im = n_re, n_im
            st[:, c_re:c_re + half] = x_re
            st[:, c_im:c_im + half] = x_im

    def output_map(kt):
        return jnp.dot(xs[:, kt * kw:(kt + 1) * kw].astype(BF16), cc_ref[kt],
                       preferred_element_type=F32)

    input_map(0)
    ys = []
    for kt in range(KT):
        if kt + 1 < KT:
            input_map(kt + 1)
        recurrence(kt)
        ys.append(output_map(kt))
    y = jnp.concatenate(ys, axis=1) + d_ref[...] * u_t
    y = 0.5 * y * (1.0 + jnp.tanh(math.sqrt(2.0 / math.pi) * (y + 0.044715 * (y * y * y))))
    y = y * _sigmoid(jnp.dot(y.astype(BF16), wg_ref[...], preferred_element_type=F32) + bg_ref[...])
    for j in range(nl):
        ytm[j] = y[:, j * LANE:(j + 1) * LANE]
    for b in range(B):
        for j in range(nl):
            y_ref[b, :, j * LANE:(j + 1) * LANE] = ytm[j, pl.ds(b, TB, stride=B), :].astype(y_ref.dtype)

    @pl.when(t == pl.num_programs(0) - 1)
    def _():
        xfin_ref[...] = st[...]


def _s5(proj3, x0, bb, cc, lam, d_row, w_glu, b_glu, TB):
    bsz, seqlen, n = proj3.shape
    width = d_row.shape[1]
    kt, cw, kw = bb.shape
    sdim = x0.shape[1]
    const = lambda shape: pl.BlockSpec(shape, lambda t: (0,) * len(shape))
    kern = functools.partial(_s5_kernel, B=bsz, TB=TB, KT=kt, SW=kw // 2)
    return pl.pallas_call(
        kern,
        grid=(seqlen // TB,),
        in_specs=[
            pl.BlockSpec((bsz, TB, width), lambda t: (0, t, n // width - 1)),
            const((bsz, sdim)), const((kt, cw, kw)), const((kt, kw, cw)), const(lam.shape),
            const((1, width)), const((width, width)), const((1, width)),
        ],
        out_specs=[pl.BlockSpec((bsz, TB, width), lambda t: (0, t, 0)), const((bsz, sdim))],
        out_shape=[jax.ShapeDtypeStruct((bsz, seqlen, width), BF16),
                   jax.ShapeDtypeStruct((bsz, sdim), F32)],
        scratch_shapes=[pltpu.VMEM((width // LANE, bsz * TB, LANE), F32),
                        pltpu.VMEM((bsz * TB, sdim), F32),
                        pltpu.VMEM((width // LANE, bsz * TB, LANE), F32),
                        pltpu.VMEM((bsz, sdim), F32)],
        compiler_params=pltpu.CompilerParams(
            dimension_semantics=("arbitrary",), vmem_limit_bytes=VMEM_LIMIT),
        name="s5",
    )(proj3, x0, bb, cc, lam, d_row, w_glu, b_glu)


def _out_ln_kernel(x_ref, o_ref, y5_ref, lng_ref, lnb_ref, wout_ref, g1_ref, b1_ref, x1_ref,
                   *, alpha):
    dn = o_ref.shape[1]
    h = _layer_norm(x_ref[...], lng_ref[...], lnb_ref[...])
    mix = (jnp.dot(o_ref[...], wout_ref[0:dn, :], preferred_element_type=F32)
           + jnp.dot(y5_ref[...], wout_ref[dn:, :], preferred_element_type=F32))
    x1_ref[...] = _layer_norm(alpha * h + mix, g1_ref[...], b1_ref[...])


def _out_ln(x2, o2, y52, ln_g, ln_b, w_out, g1, b1, alpha, tm):
    rows, d = x2.shape
    row = lambda w: pl.BlockSpec((tm, w), lambda i: (i, 0))
    vec = pl.BlockSpec((1, d), lambda i: (0, 0))
    return pl.pallas_call(
        functools.partial(_out_ln_kernel, alpha=alpha),
        grid=(rows // tm,),
        in_specs=[row(d), row(o2.shape[1]), row(y52.shape[1]), vec, vec,
                  pl.BlockSpec(w_out.shape, lambda i: (0, 0)), vec, vec],
        out_specs=row(d),
        out_shape=jax.ShapeDtypeStruct((rows, d), F32),
        compiler_params=pltpu.CompilerParams(
            dimension_semantics=("arbitrary",), vmem_limit_bytes=VMEM_LIMIT),
        name="out_ln",
    )(x2, o2, y52, ln_g, ln_b, w_out, g1, b1)


def _ffn_kernel(x1_ref, wg_ref, wu_ref, wd_ref, g2_ref, b2_ref, y_ref, x1b, *, alpha):
    j = pl.program_id(1)

    @pl.when(j == 0)
    def _():
        x1 = x1_ref[...]
        x1b[...] = x1.astype(BF16)
        y_ref[...] = alpha * x1

    xb = x1b[...]
    gate = jnp.dot(xb, wg_ref[...], preferred_element_type=F32)
    up = jnp.dot(xb, wu_ref[...], preferred_element_type=F32)
    hb = (_silu(gate) * up).astype(BF16)
    th = wd_ref.shape[0]
    for c in range(0, y_ref.shape[1], th):
        y_ref[:, c:c + th] += jnp.dot(hb, wd_ref[:, c:c + th], preferred_element_type=F32)

    @pl.when(j == pl.num_programs(1) - 1)
    def _():
        y_ref[...] = _layer_norm(y_ref[...], g2_ref[...], b2_ref[...])


def _ffn(x1, w_gate, w_up, w_down, g2, b2, alpha, tm, th):
    rows, d = x1.shape
    fh = w_gate.shape[1]
    vec = pl.BlockSpec((1, d), lambda i, j: (0, 0))
    return pl.pallas_call(
        functools.partial(_ffn_kernel, alpha=alpha),
        grid=(rows // tm, fh // th),
        in_specs=[
            pl.BlockSpec((tm, d), lambda i, j: (i, 0)),
            pl.BlockSpec((d, th), lambda i, j: (0, j)),
            pl.BlockSpec((d, th), lambda i, j: (0, j)),
            pl.BlockSpec((th, d), lambda i, j: (j, 0)),
            vec, vec,
        ],
        out_specs=pl.BlockSpec((tm, d), lambda i, j: (i, 0)),
        out_shape=jax.ShapeDtypeStruct((rows, d), F32),
        scratch_shapes=[pltpu.VMEM((tm, d), BF16)],
        compiler_params=pltpu.CompilerParams(
            dimension_semantics=("arbitrary", "arbitrary"), vmem_limit_bytes=FFN_VMEM_LIMIT),
        name="ffn",
    )(x1, w_gate, w_up, w_down, g2, b2)


def _largest_divisor(n, cap, mult):
    best = mult
    for c in range(mult, cap + 1, mult):
        if n % c == 0:
            best = c
    return best


def _layer(x3, conv_prev8, s0, x0, prm, chunk, chunks_per_step, heads_per_pack, s5_tb):
    bsz, seqlen, d = x3.shape
    rows = bsz * seqlen
    x2 = x3.reshape(rows, d)
    tm = _largest_divisor(rows, 1024, 16)
    proj, ab = _ln_inproj(x2, prm["ln_in_g"], prm["ln_in_b"], prm["w_main"], prm["w_ab"], tm,
                          _largest_divisor(prm["w_main"].shape[1], 1024, LANE))
    proj3 = proj.reshape(bsz, seqlen, -1)
    o, conv_new, s_new = _delta(proj3, ab.reshape(bsz, seqlen, LANE), conv_prev8, s0,
                                prm["cw8"], prm["gate8"], prm["nw"], chunk, chunks_per_step,
                                heads_per_pack)
    y5, x_new = _s5(proj3, x0, prm["bb"], prm["cc"], prm["lam"], prm["s5_d"], prm["w_glu"],
                    prm["b_glu"], s5_tb)
    x1 = _out_ln(x2, o.reshape(rows, -1), y5.reshape(rows, -1), prm["ln_in_g"], prm["ln_in_b"],
                 prm["w_out"], prm["ln1_g"], prm["ln1_b"], prm["alpha"],
                 _largest_divisor(rows, 512, 16))
    y = _ffn(x1, prm["w_gate"], prm["w_up"], prm["w_down"], prm["ln2_g"], prm["ln2_b"],
             prm["alpha"], tm, _largest_divisor(prm["w_gate"].shape[1], 512, LANE))
    return y.reshape(bsz, seqlen, d), conv_new, s_new, x_new


def kernel(x_prompt, x_sample, state_conv_qkv, state_delta, state_s5_re, state_s5_im, meta_tokens, ln_in_g, ln_in_b, w_in, conv_w, dn_a_log, dn_dt_bias, dn_norm_w, s5_a_re, s5_a_im, s5_log_dt, s5_b_re, s5_b_im, s5_c_re, s5_c_im, s5_d, s5_w_glu, s5_b_glu, w_out, ln1_g, ln1_b, ffn_w_gate, ffn_w_up, ffn_w_down, ln2_g, ln2_b):
    depth = w_in.shape[0]
    assert depth == 1, "single-layer trunk only"
    bp, seq, d = x_prompt.shape
    bs, dec_seq, _ = x_sample.shape
    n_meta = meta_tokens.shape[0]
    qkv_dim = conv_w.shape[-1]
    dn = qkv_dim // 3
    heads = dn // HEAD_DIM
    groups, sdim = s5_a_re.shape[1:]
    width = s5_d.shape[-1]
    assert n_meta == dec_seq and seq % CHUNK == 0 and width == groups * S5_GROUP_CH
    alpha = (2.0 * depth) ** 0.25
    row = lambda v: v.reshape(1, -1).astype(F32)

    w = w_in[0]
    n_gate = 2 * heads
    w_main = jnp.concatenate([w[:, :qkv_dim + dn], w[:, qkv_dim + dn + n_gate:]], axis=1).astype(BF16)
    w_ab = jnp.pad(w[:, qkv_dim + dn:qkv_dim + dn + n_gate], ((0, 0), (0, LANE - n_gate))).astype(BF16)
    gate8 = jnp.zeros((SUBLANE, LANE), F32)
    gate8 = gate8.at[0, :heads].set(dn_a_log[0]).at[1, :heads].set(dn_dt_bias[0])
    cw8 = jnp.pad(conv_w[0].astype(F32), ((0, SUBLANE - CONV_W), (0, 0)))

    lam_re, lam_im, f_re, f_im = _s5_disc(s5_a_re[0], s5_a_im[0], s5_log_dt[0])
    bb_re = f_re[..., None] * s5_b_re[0] - f_im[..., None] * s5_b_im[0]
    bb_im = f_re[..., None] * s5_b_im[0] + f_im[..., None] * s5_b_re[0]
    gk = GROUPS_PER_KTILE
    kt = groups // gk
    eye = jnp.eye(gk, dtype=F32)
    bb5 = jnp.stack([bb_re, bb_im]).reshape(2, kt, gk, sdim, S5_GROUP_CH)
    bb = jnp.einsum("rkgph,gG->kGhrgp", bb5, eye).reshape(kt, gk * S5_GROUP_CH, 2 * gk * sdim).astype(BF16)
    cc5 = jnp.stack([s5_c_re[0], -s5_c_im[0]]).reshape(2, kt, gk, S5_GROUP_CH, sdim)
    cc = jnp.einsum("rkghp,gG->krgpGh", cc5, eye).reshape(kt, 2 * gk * sdim, gk * S5_GROUP_CH).astype(BF16)
    lam = jnp.concatenate([lam_re.reshape(kt, gk * sdim), lam_im.reshape(kt, gk * sdim)], axis=0)

    prm = dict(
        ln_in_g=row(ln_in_g), ln_in_b=row(ln_in_b), w_main=w_main, w_ab=w_ab, cw8=cw8, gate8=gate8,
        nw=row(dn_norm_w[0]), bb=bb, cc=cc, lam=lam, s5_d=row(s5_d[0]), w_glu=s5_w_glu[0].astype(BF16),
        b_glu=row(s5_b_glu[0]), w_out=w_out[0].astype(BF16), ln1_g=row(ln1_g[0]), ln1_b=row(ln1_b[0]),
        w_gate=ffn_w_gate[0].astype(BF16), w_up=ffn_w_up[0].astype(BF16),
        w_down=ffn_w_down[0].astype(BF16), ln2_g=row(ln2_g[0]), ln2_b=row(ln2_b[0]), alpha=alpha)

    def s5_pack(re, im):
        b = re.shape[0]
        return jnp.stack([re.reshape(b, kt, gk, sdim), im.reshape(b, kt, gk, sdim)], axis=2).reshape(b, -1)

    def s5_unpack(x):
        b = x.shape[0]
        x = x.reshape(b, kt, 2, gk, sdim)
        return x[:, :, 0].reshape(b, groups, sdim), x[:, :, 1].reshape(b, groups, sdim)

    nb = 2 * SUBLANE
    assert bs + 1 <= nb
    pad_b = lambda a: jnp.pad(a.astype(F32), ((0, nb - a.shape[0]),) + ((0, 0),) * (a.ndim - 1))
    x_small = pad_b(jnp.concatenate([x_sample, meta_tokens[None].astype(x_sample.dtype)], axis=0))
    conv_small = pad_b(jnp.pad(state_conv_qkv[0], ((0, 0), (SUBLANE - CONV_W + 1, 0), (0, 0))))
    s0_small = pad_b(state_delta[0].reshape(bs, dn, HEAD_DIM))
    x0_small = pad_b(s5_pack(state_s5_re[0], state_s5_im[0]))
    y_s, conv_s, s_s, xf_s = _layer(x_small, conv_small, s0_small, x0_small, prm,
                                    dec_seq, 1, heads, dec_seq)

    bc = lambda a: jnp.broadcast_to(a[bs:bs + 1], (bp,) + a.shape[1:])
    y_p, conv_p, s_p, xf_p = _layer(x_prompt, bc(conv_s), bc(s_s), bc(xf_s), prm,
                                    CHUNK, DELTA_CHUNKS_PER_STEP, 256 // CHUNK, CHUNK)

    re_p, im_p = s5_unpack(xf_p)
    re_s, im_s = s5_unpack(xf_s[:bs])
    tail = SUBLANE - CONV_W + 1
    return (y_p.astype(x_prompt.dtype), y_s[:bs].astype(x_sample.dtype),
            conv_p[None, :, tail:].astype(state_conv_qkv.dtype),
            s_p.reshape(bp, heads, HEAD_DIM, HEAD_DIM)[None].astype(state_delta.dtype),
            re_p[None].astype(state_s5_re.dtype), im_p[None].astype(state_s5_im.dtype),
            conv_s[None, :bs, tail:].astype(state_conv_qkv.dtype),
            s_s[:bs].reshape(bs, heads, HEAD_DIM, HEAD_DIM)[None].astype(state_delta.dtype),
            re_s[None].astype(state_s5_re.dtype), im_s[None].astype(state_s5_im.dtype))
```

```python
import functools
import math

import jax
import jax.numpy as jnp
import numpy as np
from jax import lax
from jax.experimental import pallas as pl
from jax.experimental.pallas import tpu as pltpu

LN_EPS = 1e-5
RMS_EPS = 1e-6
CHUNK = 64
HEAD_DIM = 128
DELTA_CHUNKS_PER_STEP = 4
CONV_W = 4
S5_GROUP_CH = 16
GROUPS_PER_KTILE = 16
LANE = 128
SUBLANE = 8
VMEM_LIMIT = 56 * 1024 * 1024
FFN_VMEM_LIMIT = 60 * 1024 * 1024

F32 = jnp.float32
BF16 = jnp.bfloat16


def _layer_norm(x, g, b):
    mu = jnp.mean(x, axis=-1, keepdims=True)
    xc = x - mu
    var = jnp.mean(xc * xc, axis=-1, keepdims=True)
    return xc * lax.rsqrt(var + LN_EPS) * g + b


def _sigmoid(x):
    return 1.0 / (1.0 + jnp.exp(-x))


def _silu(x):
    return x * _sigmoid(x)


def _softplus(x):
    return jnp.maximum(x, 0.0) + jnp.log1p(jnp.exp(-jnp.abs(x)))


def _ln_inproj_kernel(x_ref, g_ref, b_ref, w_ref, wu_ref, wab_ref, proj_ref, ab_ref, h_scr):
    j = pl.program_id(1)
    last = pl.num_programs(1) - 1

    @pl.when(j == 0)
    def _():
        h = _layer_norm(x_ref[...], g_ref[...], b_ref[...]).astype(BF16)
        h_scr[...] = h
        ab_ref[...] = jnp.dot(h, wab_ref[...], preferred_element_type=F32)

    @pl.when(j < last)
    def _():
        proj_ref[...] = jnp.dot(h_scr[...], w_ref[...], preferred_element_type=F32)

    @pl.when(j == last)
    def _():
        proj_ref[...] = jnp.dot(h_scr[...], wu_ref[...], preferred_element_type=F32)


def _ln_inproj(x2, ln_g, ln_b, w_qkvz, w_u, w_ab, tm):
    rows, d = x2.shape
    tn = w_u.shape[1]
    na = w_qkvz.shape[1] // tn
    assert w_qkvz.shape[1] == na * tn
    return pl.pallas_call(
        _ln_inproj_kernel,
        grid=(rows // tm, na + 1),
        in_specs=[
            pl.BlockSpec((tm, d), lambda i, j: (i, 0)),
            pl.BlockSpec((1, d), lambda i, j: (0, 0)),
            pl.BlockSpec((1, d), lambda i, j: (0, 0)),
            pl.BlockSpec((d, tn), lambda i, j: (0, jnp.minimum(j, na - 1))),
            pl.BlockSpec((d, tn), lambda i, j: (0, 0)),
            pl.BlockSpec((d, LANE), lambda i, j: (0, 0)),
        ],
        out_specs=[
            pl.BlockSpec((tm, tn), lambda i, j: (i, j)),
            pl.BlockSpec((tm, LANE), lambda i, j: (i, 0)),
        ],
        out_shape=[jax.ShapeDtypeStruct((rows, (na + 1) * tn), F32),
                   jax.ShapeDtypeStruct((rows, LANE), F32)],
        scratch_shapes=[pltpu.VMEM((tm, d), BF16)],
        compiler_params=pltpu.CompilerParams(
            dimension_semantics=("arbitrary", "arbitrary"), vmem_limit_bytes=VMEM_LIMIT),
        name="ln_inproj",
    )(x2, ln_g, ln_b, w_qkvz, w_u, w_ab)


def _delta_masks(C, HP):
    R = HP * C
    ri = np.arange(R)[:, None]
    ci = np.arange(R)[None, :]
    same = (ri // C) == (ci // C)
    rr = [same & (ri >= ci)]
    s = 2
    while s < C:
        rr.append(((ri // (2 * s)) == (ci // (2 * s))) & ((ri // s) % 2 == 1) & ((ci // s) % 2 == 0))
        s *= 2
    rr.append(((ri // 2) == (ci // 2)) & (ri % 2 == 1) & (ci % 2 == 0))
    rr.append(ri == ci)
    return jnp.asarray(np.stack(rr), BF16)


def _delta_kernel(qkv_ref, z_ref, ab_ref, cprev_ref, s0_ref, cw_ref, gate_ref, nw_ref,
                  mrr_ref,
                  o_ref, cnew_ref, sfin_ref, cbuf, s_scr, *, C, NCH, HP, H):
    t = pl.program_id(1)
    dn = H * HEAD_DIM
    R = HP * C
    SR = HP * HEAD_DIM
    TB = NCH * C
    tail = SUBLANE

    @pl.when(t == 0)
    def _():
        cbuf[0:tail, :] = cprev_ref[0]
        s_scr[...] = s0_ref[0]

    cbuf[tail:tail + TB, :] = qkv_ref[0]

    def conv_cols(r0, c0):
        acc = cbuf[r0 + tail - 3:r0 + tail - 3 + C, c0:c0 + HEAD_DIM] * cw_ref[0:1, c0:c0 + HEAD_DIM]
        for i in range(1, CONV_W):
            acc = acc + (cbuf[r0 + tail - 3 + i:r0 + tail - 3 + i + C, c0:c0 + HEAD_DIM]
                         * cw_ref[i:i + 1, c0:c0 + HEAD_DIM])
        return _silu(acc)

    def l2n(x):
        return x * lax.rsqrt(jnp.sum(x * x, axis=-1, keepdims=True) + RMS_EPS)

    def block_diag(x):
        zero = jnp.zeros((C, HEAD_DIM), BF16)
        return jnp.concatenate(
            [jnp.concatenate([x[r * C:(r + 1) * C, :] if r % HP == b else zero for b in range(HP)], axis=1)
             for r in range(x.shape[0] // C)], axis=0)

    bf = lambda x: x.astype(BF16)
    mm = lambda a, b: jnp.dot(a, b, preferred_element_type=F32)
    rowc = lax.broadcasted_iota(jnp.int32, (C, LANE), 0)
    reps = LANE // C
    lane_rep = lax.broadcasted_iota(jnp.int32, (1, LANE), 1) // C

    chains = [(c, p) for c in range(NCH) for p in range(H // HP)]
    gates = {}
    for c in range(NCH):
        r0 = c * C
        ab = ab_ref[0, r0:r0 + C, :]
        g_all = -jnp.exp(gate_ref[0:1, :]) * _softplus(ab + gate_ref[1:2, :])
        beta_all = _sigmoid(ab)
        gcum_all = g_all
        s = 1
        while s < C:
            gcum_all = gcum_all + jnp.where(rowc >= s, pltpu.roll(gcum_all, s, 0), 0.0)
            s *= 2
        gates[c] = (beta_all, gcum_all, gcum_all[C - 1:C, :],
                    jnp.concatenate([gcum_all] * reps, axis=0).T)

    st = {}

    def prologue(c, p):
        r0 = c * C
        beta_all, gcum_all, glast_all, g_t = gates[c]
        heads = range(p * HP, (p + 1) * HP)
        qs = jnp.concatenate(
            [l2n(conv_cols(r0, h * HEAD_DIM)) * (HEAD_DIM ** -0.5) for h in heads], axis=0)
        ks = jnp.concatenate([l2n(conv_cols(r0, dn + h * HEAD_DIM)) for h in heads], axis=0)
        vs = jnp.concatenate([conv_cols(r0, 2 * dn + h * HEAD_DIM) for h in heads], axis=0)
        beta_c = jnp.concatenate([beta_all[:, H + h:H + h + 1] for h in heads], axis=0)
        gc_c = jnp.concatenate([gcum_all[:, h:h + 1] for h in heads], axis=0)
        gl_c = jnp.concatenate(
            [jnp.broadcast_to(glast_all[:, h:h + 1], (C, 1)) for h in heads], axis=0)
        row_blocks = []
        for m in range(R // LANE):
            blk = g_t[p * HP + m * reps:p * HP + m * reps + 1, :]
            for r in range(1, reps):
                hr = p * HP + m * reps + r
                blk = jnp.where(lane_rep == r, g_t[hr:hr + 1, :], blk)
            row_blocks.append(blk)
        gc_r = jnp.concatenate(row_blocks, axis=1)
        kb = ks * beta_c
        gexp = jnp.exp(gc_c)
        g_tot = jnp.concatenate(
            [jnp.broadcast_to(jnp.exp(glast_all[:, h:h + 1]), (HEAD_DIM, 1)) for h in heads], axis=0)
        st[c, p] = dict(
            gc_c=gc_c, gc_r=gc_r, lhs=bf(jnp.concatenate([kb, qs], axis=0)), ks_b=bf(ks),
            rhs=bf(jnp.concatenate([vs * beta_c, kb * gexp], axis=1)), q_dec=qs * gexp,
            kdec_bd=block_diag(bf(ks * jnp.exp(gl_c - gc_c))), g_tot=g_tot)

    def gram_issue(chs):
        for ch in chs:
            d = st[ch]
            d["kkqk"] = lax.dot_general(d["lhs"], d["ks_b"], (((1,), (1,)), ((), ())),
                                        preferred_element_type=F32)

    n_lvl = mrr_ref.shape[0] - 2

    def lane_tiles(tiles):
        zero = jnp.zeros((LANE, LANE), BF16)
        return jnp.concatenate(
            [jnp.concatenate([t if a == b else zero for b in range(len(tiles))], axis=1)
             for a, t in enumerate(tiles)], axis=0)

    def gram_post(chs):
        for ch in chs:
            d = st[ch]
            a_t, qk_t = [], []
            for m in range(R // LANE):
                rows = slice(m * LANE, (m + 1) * LANE)
                e = jnp.exp(jnp.minimum(d["gc_c"][rows, :] - d["gc_r"][:, rows], 0.0))
                a_t.append(bf(d["kkqk"][rows, rows] * e))
                qk_t.append(bf(d["kkqk"][R + m * LANE:R + (m + 1) * LANE, rows] * e)
                            * mrr_ref[0, rows, rows])
            d["a_t"] = a_t
            d["qk_b"] = lane_tiles(qk_t)
            d["x_b"] = mrr_ref[n_lvl + 1] - lane_tiles(
                [t * mrr_ref[n_lvl, m * LANE:(m + 1) * LANE, m * LANE:(m + 1) * LANE]
                 for m, t in enumerate(a_t)])

    def level_first(chs, lvl):
        for ch in chs:
            d = st[ch]
            d["p"] = mm(d["x_b"], lane_tiles(
                [t * mrr_ref[lvl, m * LANE:(m + 1) * LANE, m * LANE:(m + 1) * LANE]
                 for m, t in enumerate(d["a_t"])]))

    def level_second(chs):
        for ch in chs:
            d = st[ch]
            d["q"] = mm(bf(d["p"]), d["x_b"])

    def level_post(chs):
        for ch in chs:
            d = st[ch]
            d["x_b"] = d["x_b"] - bf(d["q"])

    def solve_issue(chs):
        for ch in chs:
            d = st[ch]
            d["sol"] = mm(d["x_b"], d["rhs"])

    def solve_post(chs):
        for ch in chs:
            d = st[ch]
            d["kq_bd"] = block_diag(bf(jnp.concatenate([d["sol"][:, HEAD_DIM:], d["q_dec"]], axis=0)))

    packs = range(H // HP)
    rec = {}

    def rec_read(c):
        for p in packs:
            s_old = s_scr[p * SR:(p + 1) * SR, :]
            rec[c, p] = dict(s_old=s_old, ks_qs=mm(st[c, p]["kq_bd"], bf(s_old)))

    def rec_inner(c):
        for p in packs:
            d = rec[c, p]
            d["w_b"] = bf(st[c, p]["sol"][:, :HEAD_DIM] - d["ks_qs"][:R])
            d["o"] = mm(st[c, p]["qk_b"], d["w_b"])

    def rec_write(c):
        for p in packs:
            d = rec[c, p]
            s_scr[p * SR:(p + 1) * SR, :] = d["s_old"] * st[c, p]["g_tot"] + lax.dot_general(
                st[c, p]["kdec_bd"], d["w_b"], (((0,), (0,)), ((), ())), preferred_element_type=F32)

    def rec_out(c):
        r0 = c * C
        for p in packs:
            d = rec[c, p]
            o_p = d["ks_qs"][R:] + d["o"]
            for i in range(HP):
                h = p * HP + i
                oh = o_p[i * C:(i + 1) * C, :]
                oh = oh * lax.rsqrt(jnp.mean(oh * oh, axis=-1, keepdims=True) + RMS_EPS) * nw_ref[...]
                zz = z_ref[0, r0:r0 + C, h * HEAD_DIM:(h + 1) * HEAD_DIM]
                o_ref[0, r0:r0 + C, h * HEAD_DIM:(h + 1) * HEAD_DIM] = (oh * _silu(zz)).astype(o_ref.dtype)

    group = max(NCH // 2, 1) * len(packs)
    groups = [chains[g:g + group] for g in range(0, len(chains), group)]
    fillers = []
    for ch in groups[0]:
        prologue(*ch)
    for g, chs in enumerate(groups):
        if g + 1 < len(groups):
            fillers = [functools.partial(prologue, *ch) for ch in groups[g + 1]] + fillers
        fill = lambda: fillers.pop(0)() if fillers else None
        gram_issue(chs)
        gram_post(chs)
        fill()
        for lvl in range(1, n_lvl):
            level_first(chs, lvl)
            level_second(chs)
            level_post(chs)
            fill()
        solve_issue(chs)
        solve_post(chs)
        while fillers:
            fill()
        for c in sorted({c for c, _ in chs}):
            fillers += [functools.partial(rec_read, c), functools.partial(rec_inner, c),
                        lambda c=c: (rec_write(c), rec_out(c))]
    while fillers:
        fill()

    cbuf[0:tail, :] = cbuf[TB:TB + tail, :]
    cnew_ref[0] = cbuf[0:tail, :]

    @pl.when(t == pl.num_programs(1) - 1)
    def _():
        sfin_ref[0] = s_scr[...]


def _delta(proj3, ab3, conv_prev, s0, cw8, gate8, nw, C, NCH, HP):
    bsz, seqlen, _ = proj3.shape
    qkv_dim = cw8.shape[1]
    dn = qkv_dim // 3
    heads = dn // HEAD_DIM
    tb = NCH * C
    kern = functools.partial(_delta_kernel, C=C, NCH=NCH, HP=HP, H=heads)
    masks = _delta_masks(C, HP)
    const = lambda a: pl.BlockSpec(a.shape, lambda b, t: (0,) * a.ndim)
    return pl.pallas_call(
        kern,
        grid=(bsz, seqlen // tb),
        in_specs=[
            pl.BlockSpec((1, tb, qkv_dim), lambda b, t: (b, t, 0)),
            pl.BlockSpec((1, tb, dn), lambda b, t: (b, t, qkv_dim // dn)),
            pl.BlockSpec((1, tb, LANE), lambda b, t: (b, t, 0)),
            pl.BlockSpec((1, SUBLANE, qkv_dim), lambda b, t: (b, 0, 0)),
            pl.BlockSpec((1, dn, HEAD_DIM), lambda b, t: (b, 0, 0)),
            pl.BlockSpec((SUBLANE, qkv_dim), lambda b, t: (0, 0)),
            pl.BlockSpec((SUBLANE, LANE), lambda b, t: (0, 0)),
            pl.BlockSpec((1, HEAD_DIM), lambda b, t: (0, 0)),
            const(masks),
        ],
        out_specs=[
            pl.BlockSpec((1, tb, dn), lambda b, t: (b, t, 0)),
            pl.BlockSpec((1, SUBLANE, qkv_dim), lambda b, t: (b, 0, 0)),
            pl.BlockSpec((1, dn, HEAD_DIM), lambda b, t: (b, 0, 0)),
        ],
        out_shape=[
            jax.ShapeDtypeStruct((bsz, seqlen, dn), BF16),
            jax.ShapeDtypeStruct((bsz, SUBLANE, qkv_dim), F32),
            jax.ShapeDtypeStruct((bsz, dn, HEAD_DIM), F32),
        ],
        scratch_shapes=[pltpu.VMEM((SUBLANE + tb, qkv_dim), F32), pltpu.VMEM((dn, HEAD_DIM), F32)],
        compiler_params=pltpu.CompilerParams(
            dimension_semantics=("arbitrary", "arbitrary"), vmem_limit_bytes=VMEM_LIMIT),
        name="delta_rule",
    )(proj3, proj3, ab3, conv_prev, s0, cw8, gate8, nw, masks)


def _s5_disc_kernel(are_ref, aim_ref, logdt_ref, lre_ref, lim_ref, fre_ref, fim_ref):
    a_re = are_ref[...]
    a_im = aim_ref[...]
    dt = jnp.exp(logdt_ref[...])
    mag = jnp.exp(a_re * dt)
    lam_re = mag * jnp.cos(a_im * dt)
    lam_im = mag * jnp.sin(a_im * dt)
    den = a_re * a_re + a_im * a_im
    nr = lam_re - 1.0
    lre_ref[...] = lam_re
    lim_ref[...] = lam_im
    fre_ref[...] = (nr * a_re + lam_im * a_im) / den
    fim_ref[...] = (lam_im * a_re - nr * a_im) / den


def _s5_disc(a_re, a_im, log_dt):
    shp = jax.ShapeDtypeStruct(a_re.shape, F32)
    return pl.pallas_call(_s5_disc_kernel, out_shape=[shp] * 4, name="s5_disc")(
        a_re, a_im, log_dt.reshape(-1, 1))


def _s5_kernel(u_ref, x0_ref, bb_ref, cc_ref, lam_ref, d_ref, wg_ref, bg_ref,
               y_ref, xfin_ref, utm, xs, ytm, st, *, B, TB, KT, SW):
    t = pl.program_id(0)
    kw = 2 * SW
    cw = u_ref.shape[2] // KT

    @pl.when(t == 0)
    def _():
        st[...] = x0_ref[...]

    nl = u_ref.shape[2] // LANE
    for b in range(B):
        for j in range(nl):
            utm[j, pl.ds(b, TB, stride=B), :] = u_ref[b, :, j * LANE:(j + 1) * LANE]
    u_t = jnp.concatenate([utm[j] for j in range(nl)], axis=1)
    ub = u_t.astype(BF16)

    def input_map(kt):
        xs[:, kt * kw:(kt + 1) * kw] = jnp.dot(ub[:, kt * cw:(kt + 1) * cw], bb_ref[kt],
                                               preferred_element_type=F32)

    def recurrence(kt):
        half = SW // 2
        for hf in range(2):
            c_re = kt * kw + hf * half
            c_im = c_re + SW
            l_re = jnp.broadcast_to(lam_ref[kt:kt + 1, hf * half:(hf + 1) * half], (B, half))
            l_im = jnp.broadcast_to(lam_ref[KT + kt:KT + kt + 1, hf * half:(hf + 1) * half], (B, half))
            x_re = st[:, c_re:c_re + half]
            x_im = st[:, c_im:c_im + half]
            for tt in range(TB):
                rows = slice(tt * B, (tt + 1) * B)
                n_re = l_re * x_re - l_im * x_im + xs[rows, c_re:c_re + half]
                n_im = l_re * x_im + l_im * x_re + xs[rows, c_im:c_im + half]
                xs[rows, c_re:c_re + half] = n_re
                xs[rows, c_im:c_im + half] = n_im
                x_re, x_im = n_re, n_im
            st[:, c_re:c_re + half] = x_re
            st[:, c_im:c_im + half] = x_im

    def output_map(kt):
        return jnp.dot(xs[:, kt * kw:(kt + 1) * kw].astype(BF16), cc_ref[kt],
                       preferred_element_type=F32)

    input_map(0)
    ys = []
    for kt in range(KT):
        if kt + 1 < KT:
            input_map(kt + 1)
        recurrence(kt)
        ys.append(output_map(kt))
    y = jnp.concatenate(ys, axis=1) + d_ref[...] * u_t
    y = 0.5 * y * (1.0 + jnp.tanh(math.sqrt(2.0 / math.pi) * (y + 0.044715 * (y * y * y))))
    y = y * _sigmoid(jnp.dot(y.astype(BF16), wg_ref[...], preferred_element_type=F32) + bg_ref[...])
    for j in range(nl):
        ytm[j] = y[:, j * LANE:(j + 1) * LANE]
    for b in range(B):
        for j in range(nl):
            y_ref[b, :, j * LANE:(j + 1) * LANE] = ytm[j, pl.ds(b, TB, stride=B), :].astype(y_ref.dtype)

    @pl.when(t == pl.num_programs(0) - 1)
    def _():
        xfin_ref[...] = st[...]


def _s5(proj3, x0, bb, cc, lam, d_row, w_glu, b_glu, TB):
    bsz, seqlen, n = proj3.shape
    width = d_row.shape[1]
    kt, cw, kw = bb.shape
    sdim = x0.shape[1]
    const = lambda shape: pl.BlockSpec(shape, lambda t: (0,) * len(shape))
    kern = functools.partial(_s5_kernel, B=bsz, TB=TB, KT=kt, SW=kw // 2)
    return pl.pallas_call(
        kern,
        grid=(seqlen // TB,),
        in_specs=[
            pl.BlockSpec((bsz, TB, width), lambda t: (0, t, n // width - 1)),
            const((bsz, sdim)), const((kt, cw, kw)), const((kt, kw, cw)), const(lam.shape),
            const((1, width)), const((width, width)), const((1, width)),
        ],
        out_specs=[pl.BlockSpec((bsz, TB, width), lambda t: (0, t, 0)), const((bsz, sdim))],
        out_shape=[jax.ShapeDtypeStruct((bsz, seqlen, width), BF16),
                   jax.ShapeDtypeStruct((bsz, sdim), F32)],
        scratch_shapes=[pltpu.VMEM((width // LANE, bsz * TB, LANE), F32),
                        pltpu.VMEM((bsz * TB, sdim), F32),
                        pltpu.VMEM((width // LANE, bsz * TB, LANE), F32),
                        pltpu.VMEM((bsz, sdim), F32)],
        compiler_params=pltpu.CompilerParams(
            dimension_semantics=("arbitrary",), vmem_limit_bytes=VMEM_LIMIT),
        name="s5",
    )(proj3, x0, bb, cc, lam, d_row, w_glu, b_glu)


def _out_ln_kernel(x_ref, o_ref, y5_ref, lng_ref, lnb_ref, wout_ref, g1_ref, b1_ref, x1_ref,
                   *, alpha):
    dn = o_ref.shape[1]
    h = _layer_norm(x_ref[...], lng_ref[...], lnb_ref[...])
    mix = (jnp.dot(o_ref[...], wout_ref[0:dn, :], preferred_element_type=F32)
           + jnp.dot(y5_ref[...], wout_ref[dn:, :], preferred_element_type=F32))
    x1_ref[...] = _layer_norm(alpha * h + mix, g1_ref[...], b1_ref[...])


def _out_ln(x2, o2, y52, ln_g, ln_b, w_out, g1, b1, alpha, tm):
    rows, d = x2.shape
    row = lambda w: pl.BlockSpec((tm, w), lambda i: (i, 0))
    vec = pl.BlockSpec((1, d), lambda i: (0, 0))
    return pl.pallas_call(
        functools.partial(_out_ln_kernel, alpha=alpha),
        grid=(rows // tm,),
        in_specs=[row(d), row(o2.shape[1]), row(y52.shape[1]), vec, vec,
                  pl.BlockSpec(w_out.shape, lambda i: (0, 0)), vec, vec],
        out_specs=row(d),
        out_shape=jax.ShapeDtypeStruct((rows, d), F32),
        compiler_params=pltpu.CompilerParams(
            dimension_semantics=("arbitrary",), vmem_limit_bytes=VMEM_LIMIT),
        name="out_ln",
    )(x2, o2, y52, ln_g, ln_b, w_out, g1, b1)


def _ffn_kernel(x1_ref, wg_ref, wu_ref, wd_ref, g2_ref, b2_ref, y_ref, x1b, *, alpha):
    j = pl.program_id(1)

    @pl.when(j == 0)
    def _():
        x1 = x1_ref[...]
        x1b[...] = x1.astype(BF16)
        y_ref[...] = alpha * x1

    xb = x1b[...]
    gate = jnp.dot(xb, wg_ref[...], preferred_element_type=F32)
    up = jnp.dot(xb, wu_ref[...], preferred_element_type=F32)
    hb = (_silu(gate) * up).astype(BF16)
    th = wd_ref.shape[0]
    for c in range(0, y_ref.shape[1], th):
        y_ref[:, c:c + th] += jnp.dot(hb, wd_ref[:, c:c + th], preferred_element_type=F32)

    @pl.when(j == pl.num_programs(1) - 1)
    def _():
        y_ref[...] = _layer_norm(y_ref[...], g2_ref[...], b2_ref[...])


def _ffn(x1, w_gate, w_up, w_down, g2, b2, alpha, tm, th):
    rows, d = x1.shape
    fh = w_gate.shape[1]
    vec = pl.BlockSpec((1, d), lambda i, j: (0, 0))
    return pl.pallas_call(
        functools.partial(_ffn_kernel, alpha=alpha),
        grid=(rows // tm, fh // th),
        in_specs=[
            pl.BlockSpec((tm, d), lambda i, j: (i, 0)),
            pl.BlockSpec((d, th), lambda i, j: (0, j)),
            pl.BlockSpec((d, th), lambda i, j: (0, j)),
            pl.BlockSpec((th, d), lambda i, j: (j, 0)),
            vec, vec,
        ],
        out_specs=pl.BlockSpec((tm, d), lambda i, j: (i, 0)),
        out_shape=jax.ShapeDtypeStruct((rows, d), F32),
        scratch_shapes=[pltpu.VMEM((tm, d), BF16)],
        compiler_params=pltpu.CompilerParams(
            dimension_semantics=("arbitrary", "arbitrary"), vmem_limit_bytes=FFN_VMEM_LIMIT),
        name="ffn",
    )(x1, w_gate, w_up, w_down, g2, b2)


def _largest_divisor(n, cap, mult):
    best = mult
    for c in range(mult, cap + 1, mult):
        if n % c == 0:
            best = c
    return best


def _layer(x3, conv_prev8, s0, x0, prm, chunk, chunks_per_step, heads_per_pack, s5_tb):
    bsz, seqlen, d = x3.shape
    rows = bsz * seqlen
    x2 = x3.reshape(rows, d)
    tm = _largest_divisor(rows, 1024, 16)
    proj, ab = _ln_inproj(x2, prm["ln_in_g"], prm["ln_in_b"], prm["w_qkvz"], prm["w_u"], prm["w_ab"], tm)
    proj3 = proj.reshape(bsz, seqlen, -1)
    o, conv_new, s_new = _delta(proj3, ab.reshape(bsz, seqlen, LANE), conv_prev8, s0,
                                prm["cw8"], prm["gate8"], prm["nw"], chunk, chunks_per_step,
                                heads_per_pack)
    y5, x_new = _s5(proj3, x0, prm["bb"], prm["cc"], prm["lam"], prm["s5_d"], prm["w_glu"],
                    prm["b_glu"], s5_tb)
    x1 = _out_ln(x2, o.reshape(rows, -1), y5.reshape(rows, -1), prm["ln_in_g"], prm["ln_in_b"],
                 prm["w_out"], prm["ln1_g"], prm["ln1_b"], prm["alpha"],
                 _largest_divisor(rows, 512, 16))
    y = _ffn(x1, prm["w_gate"], prm["w_up"], prm["w_down"], prm["ln2_g"], prm["ln2_b"],
             prm["alpha"], tm, _largest_divisor(prm["w_gate"].shape[1], 512, LANE))
    return y.reshape(bsz, seqlen, d), conv_new, s_new, x_new


def kernel(x_prompt, x_sample, state_conv_qkv, state_delta, state_s5_re, state_s5_im, meta_tokens, ln_in_g, ln_in_b, w_in, conv_w, dn_a_log, dn_dt_bias, dn_norm_w, s5_a_re, s5_a_im, s5_log_dt, s5_b_re, s5_b_im, s5_c_re, s5_c_im, s5_d, s5_w_glu, s5_b_glu, w_out, ln1_g, ln1_b, ffn_w_gate, ffn_w_up, ffn_w_down, ln2_g, ln2_b):
    depth = w_in.shape[0]
    assert depth == 1, "single-layer trunk only"
    bp, seq, d = x_prompt.shape
    bs, dec_seq, _ = x_sample.shape
    n_meta = meta_tokens.shape[0]
    qkv_dim = conv_w.shape[-1]
    dn = qkv_dim // 3
    heads = dn // HEAD_DIM
    groups, sdim = s5_a_re.shape[1:]
    width = s5_d.shape[-1]
    assert n_meta == dec_seq and seq % CHUNK == 0 and width == groups * S5_GROUP_CH
    alpha = (2.0 * depth) ** 0.25
    row = lambda v: v.reshape(1, -1).astype(F32)

    w = w_in[0]
    n_gate = 2 * heads
    w_qkvz = w[:, :qkv_dim + dn].astype(BF16)
    w_u = w[:, qkv_dim + dn + n_gate:].astype(BF16)
    w_ab = jnp.pad(w[:, qkv_dim + dn:qkv_dim + dn + n_gate], ((0, 0), (0, LANE - n_gate))).astype(BF16)
    gate8 = jnp.zeros((SUBLANE, LANE), F32)
    gate8 = gate8.at[0, :heads].set(dn_a_log[0]).at[1, :heads].set(dn_dt_bias[0])
    cw8 = jnp.pad(conv_w[0].astype(F32), ((0, SUBLANE - CONV_W), (0, 0)))

    lam_re, lam_im, f_re, f_im = _s5_disc(s5_a_re[0], s5_a_im[0], s5_log_dt[0])
    bb_re = f_re[..., None] * s5_b_re[0] - f_im[..., None] * s5_b_im[0]
    bb_im = f_re[..., None] * s5_b_im[0] + f_im[..., None] * s5_b_re[0]
    gk = GROUPS_PER_KTILE
    kt = groups // gk
    eye = jnp.eye(gk, dtype=F32)
    bb5 = jnp.stack([bb_re, bb_im]).reshape(2, kt, gk, sdim, S5_GROUP_CH)
    bb = jnp.einsum("rkgph,gG->kGhrgp", bb5, eye).reshape(kt, gk * S5_GROUP_CH, 2 * gk * sdim).astype(BF16)
    cc5 = jnp.stack([s5_c_re[0], -s5_c_im[0]]).reshape(2, kt, gk, S5_GROUP_CH, sdim)
    cc = jnp.einsum("rkghp,gG->krgpGh", cc5, eye).reshape(kt, 2 * gk * sdim, gk * S5_GROUP_CH).astype(BF16)
    lam = jnp.concatenate([lam_re.reshape(kt, gk * sdim), lam_im.reshape(kt, gk * sdim)], axis=0)

    prm = dict(
        ln_in_g=row(ln_in_g), ln_in_b=row(ln_in_b), w_qkvz=w_qkvz, w_u=w_u, w_ab=w_ab, cw8=cw8, gate8=gate8,
        nw=row(dn_norm_w[0]), bb=bb, cc=cc, lam=lam, s5_d=row(s5_d[0]), w_glu=s5_w_glu[0].astype(BF16),
        b_glu=row(s5_b_glu[0]), w_out=w_out[0].astype(BF16), ln1_g=row(ln1_g[0]), ln1_b=row(ln1_b[0]),
        w_gate=ffn_w_gate[0].astype(BF16), w_up=ffn_w_up[0].astype(BF16),
        w_down=ffn_w_down[0].astype(BF16), ln2_g=row(ln2_g[0]), ln2_b=row(ln2_b[0]), alpha=alpha)

    def s5_pack(re, im):
        b = re.shape[0]
        return jnp.stack([re.reshape(b, kt, gk, sdim), im.reshape(b, kt, gk, sdim)], axis=2).reshape(b, -1)

    def s5_unpack(x):
        b = x.shape[0]
        x = x.reshape(b, kt, 2, gk, sdim)
        return x[:, :, 0].reshape(b, groups, sdim), x[:, :, 1].reshape(b, groups, sdim)

    nb = 2 * SUBLANE
    assert bs + 1 <= nb
    pad_b = lambda a: jnp.pad(a.astype(F32), ((0, nb - a.shape[0]),) + ((0, 0),) * (a.ndim - 1))
    x_small = pad_b(jnp.concatenate([x_sample, meta_tokens[None].astype(x_sample.dtype)], axis=0))
    conv_small = pad_b(jnp.pad(state_conv_qkv[0], ((0, 0), (SUBLANE - CONV_W + 1, 0), (0, 0))))
    s0_small = pad_b(state_delta[0].reshape(bs, dn, HEAD_DIM))
    x0_small = pad_b(s5_pack(state_s5_re[0], state_s5_im[0]))
    y_s, conv_s, s_s, xf_s = _layer(x_small, conv_small, s0_small, x0_small, prm,
                                    dec_seq, 1, heads, dec_seq)

    bc = lambda a: jnp.broadcast_to(a[bs:bs + 1], (bp,) + a.shape[1:])
    y_p, conv_p, s_p, xf_p = _layer(x_prompt, bc(conv_s), bc(s_s), bc(xf_s), prm,
                                    CHUNK, DELTA_CHUNKS_PER_STEP, 256 // CHUNK, CHUNK)

    re_p, im_p = s5_unpack(xf_p)
    re_s, im_s = s5_unpack(xf_s[:bs])
    tail = SUBLANE - CONV_W + 1
    return (y_p.astype(x_prompt.dtype), y_s[:bs].astype(x_sample.dtype),
            conv_p[None, :, tail:].astype(state_conv_qkv.dtype),
            s_p.reshape(bp, heads, HEAD_DIM, HEAD_DIM)[None].astype(state_delta.dtype),
            re_p[None].astype(state_s5_re.dtype), im_p[None].astype(state_s5_im.dtype),
            conv_s[None, :bs, tail:].astype(state_conv_qkv.dtype),
            s_s[:bs].reshape(bs, heads, HEAD_DIM, HEAD_DIM)[None].astype(state_delta.dtype),
            re_s[None].astype(state_s5_re.dtype), im_s[None].astype(state_s5_im.dtype))
```

```python
import functools
import math

import jax
import jax.numpy as jnp
import numpy as np
from jax import lax
from jax.experimental import pallas as pl
from jax.experimental.pallas import tpu as pltpu

LN_EPS = 1e-5
RMS_EPS = 1e-6
CHUNK = 64
HEAD_DIM = 128
DELTA_CHUNKS_PER_STEP = 4
CONV_W = 4
S5_GROUP_CH = 16
GROUPS_PER_KTILE = 16
LANE = 128
SUBLANE = 8
VMEM_LIMIT = 56 * 1024 * 1024
FFN_VMEM_LIMIT = 60 * 1024 * 1024

F32 = jnp.float32
BF16 = jnp.bfloat16


def _layer_norm(x, g, b):
    mu = jnp.mean(x, axis=-1, keepdims=True)
    xc = x - mu
    var = jnp.mean(xc * xc, axis=-1, keepdims=True)
    return xc * lax.rsqrt(var + LN_EPS) * g + b


def _sigmoid(x):
    return 0.5 + 0.5 * jnp.tanh(0.5 * x)


def _silu(x):
    h = 0.5 * x
    return h + h * jnp.tanh(h)


def _softplus(x):
    return jnp.maximum(x, 0.0) + jnp.log1p(jnp.exp(-jnp.abs(x)))


def _ln_inproj_kernel(x_ref, g_ref, b_ref, w_ref, wu_ref, wab_ref, proj_ref, ab_ref, h_scr):
    j = pl.program_id(1)
    last = pl.num_programs(1) - 1

    @pl.when(j == 0)
    def _():
        h = _layer_norm(x_ref[...], g_ref[...], b_ref[...]).astype(BF16)
        h_scr[...] = h
        ab_ref[...] = jnp.dot(h, wab_ref[...], preferred_element_type=F32)

    @pl.when(j < last)
    def _():
        proj_ref[...] = jnp.dot(h_scr[...], w_ref[...], preferred_element_type=F32)

    @pl.when(j == last)
    def _():
        proj_ref[...] = jnp.dot(h_scr[...], wu_ref[...], preferred_element_type=F32)


def _ln_inproj(x2, ln_g, ln_b, w_qkvz, w_u, w_ab, tm):
    rows, d = x2.shape
    tn = w_u.shape[1]
    na = w_qkvz.shape[1] // tn
    assert w_qkvz.shape[1] == na * tn
    return pl.pallas_call(
        _ln_inproj_kernel,
        grid=(rows // tm, na + 1),
        in_specs=[
            pl.BlockSpec((tm, d), lambda i, j: (i, 0)),
            pl.BlockSpec((1, d), lambda i, j: (0, 0)),
            pl.BlockSpec((1, d), lambda i, j: (0, 0)),
            pl.BlockSpec((d, tn), lambda i, j: (0, jnp.minimum(j, na - 1))),
            pl.BlockSpec((d, tn), lambda i, j: (0, 0)),
            pl.BlockSpec((d, LANE), lambda i, j: (0, 0)),
        ],
        out_specs=[
            pl.BlockSpec((tm, tn), lambda i, j: (i, j)),
            pl.BlockSpec((tm, LANE), lambda i, j: (i, 0)),
        ],
        out_shape=[jax.ShapeDtypeStruct((rows, (na + 1) * tn), F32),
                   jax.ShapeDtypeStruct((rows, LANE), F32)],
        scratch_shapes=[pltpu.VMEM((tm, d), BF16)],
        compiler_params=pltpu.CompilerParams(
            dimension_semantics=("arbitrary", "arbitrary"), vmem_limit_bytes=VMEM_LIMIT),
        name="ln_inproj",
    )(x2, ln_g, ln_b, w_qkvz, w_u, w_ab)


def _delta_masks(C, HP):
    R = HP * C
    ri = np.arange(R)[:, None]
    ci = np.arange(R)[None, :]
    same = (ri // C) == (ci // C)
    rr = [same & (ri >= ci)]
    s = 2
    while s < C:
        rr.append(((ri // (2 * s)) == (ci // (2 * s))) & ((ri // s) % 2 == 1) & ((ci // s) % 2 == 0))
        s *= 2
    rr.append(((ri // 2) == (ci // 2)) & (ri % 2 == 1) & (ci % 2 == 0))
    rr.append(ri == ci)
    return jnp.asarray(np.stack(rr), BF16)


def _delta_kernel(qkv_ref, z_ref, ab_ref, cprev_ref, s0_ref, cw_ref, gate_ref, nw_ref,
                  mrr_ref,
                  o_ref, cnew_ref, sfin_ref, cbuf, s_scr, *, C, NCH, HP, H):
    t = pl.program_id(1)
    dn = H * HEAD_DIM
    R = HP * C
    SR = HP * HEAD_DIM
    TB = NCH * C
    tail = SUBLANE

    @pl.when(t == 0)
    def _():
        cbuf[0:tail, :] = cprev_ref[0]
        s_scr[...] = s0_ref[0]

    cbuf[tail:tail + TB, :] = qkv_ref[0]

    def conv_cols(r0, c0):
        acc = cbuf[r0 + tail - 3:r0 + tail - 3 + C, c0:c0 + HEAD_DIM] * cw_ref[0:1, c0:c0 + HEAD_DIM]
        for i in range(1, CONV_W):
            acc = acc + (cbuf[r0 + tail - 3 + i:r0 + tail - 3 + i + C, c0:c0 + HEAD_DIM]
                         * cw_ref[i:i + 1, c0:c0 + HEAD_DIM])
        return _silu(acc)

    def l2n(x):
        return x * lax.rsqrt(jnp.sum(x * x, axis=-1, keepdims=True) + RMS_EPS)

    def block_diag(x):
        zero = jnp.zeros((C, HEAD_DIM), BF16)
        return jnp.concatenate(
            [jnp.concatenate([x[r * C:(r + 1) * C, :] if r % HP == b else zero for b in range(HP)], axis=1)
             for r in range(x.shape[0] // C)], axis=0)

    bf = lambda x: x.astype(BF16)
    mm = lambda a, b: jnp.dot(a, b, preferred_element_type=F32)
    rowc = lax.broadcasted_iota(jnp.int32, (C, LANE), 0)
    reps = LANE // C
    lane_rep = lax.broadcasted_iota(jnp.int32, (1, LANE), 1) // C

    chains = [(c, p) for c in range(NCH) for p in range(H // HP)]
    gates = {}
    for c in range(NCH):
        r0 = c * C
        ab = ab_ref[0, r0:r0 + C, :]
        g_all = -jnp.exp(gate_ref[0:1, :]) * _softplus(ab + gate_ref[1:2, :])
        beta_all = _sigmoid(ab)
        gcum_all = g_all
        s = 1
        while s < C:
            gcum_all = gcum_all + jnp.where(rowc >= s, pltpu.roll(gcum_all, s, 0), 0.0)
            s *= 2
        gates[c] = (beta_all, gcum_all, gcum_all[C - 1:C, :],
                    jnp.concatenate([gcum_all] * reps, axis=0).T)

    st = {}

    def prologue(c, p):
        r0 = c * C
        beta_all, gcum_all, glast_all, g_t = gates[c]
        heads = range(p * HP, (p + 1) * HP)
        qs = jnp.concatenate(
            [l2n(conv_cols(r0, h * HEAD_DIM)) * (HEAD_DIM ** -0.5) for h in heads], axis=0)
        ks = jnp.concatenate([l2n(conv_cols(r0, dn + h * HEAD_DIM)) for h in heads], axis=0)
        vs = jnp.concatenate([conv_cols(r0, 2 * dn + h * HEAD_DIM) for h in heads], axis=0)
        beta_c = jnp.concatenate([beta_all[:, H + h:H + h + 1] for h in heads], axis=0)
        gc_c = jnp.concatenate([gcum_all[:, h:h + 1] for h in heads], axis=0)
        gl_c = jnp.concatenate(
            [jnp.broadcast_to(glast_all[:, h:h + 1], (C, 1)) for h in heads], axis=0)
        row_blocks = []
        for m in range(R // LANE):
            blk = g_t[p * HP + m * reps:p * HP + m * reps + 1, :]
            for r in range(1, reps):
                hr = p * HP + m * reps + r
                blk = jnp.where(lane_rep == r, g_t[hr:hr + 1, :], blk)
            row_blocks.append(blk)
        gc_r = jnp.concatenate(row_blocks, axis=1)
        kb = ks * beta_c
        gexp = jnp.exp(gc_c)
        g_tot = jnp.concatenate(
            [jnp.broadcast_to(jnp.exp(glast_all[:, h:h + 1]), (HEAD_DIM, 1)) for h in heads], axis=0)
        st[c, p] = dict(
            gc_c=gc_c, gc_r=gc_r, lhs=bf(jnp.concatenate([kb, qs], axis=0)), ks_b=bf(ks),
            rhs=bf(jnp.concatenate([vs * beta_c, kb * gexp], axis=1)), q_dec=qs * gexp,
            kdec_bd=block_diag(bf(ks * jnp.exp(gl_c - gc_c))), g_tot=g_tot)

    def gram_issue(chs):
        for ch in chs:
            d = st[ch]
            d["kkqk"] = lax.dot_general(d["lhs"], d["ks_b"], (((1,), (1,)), ((), ())),
                                        preferred_element_type=F32)

    n_lvl = mrr_ref.shape[0] - 2

    def lane_tiles(tiles):
        zero = jnp.zeros((LANE, LANE), BF16)
        return jnp.concatenate(
            [jnp.concatenate([t if a == b else zero for b in range(len(tiles))], axis=1)
             for a, t in enumerate(tiles)], axis=0)

    def gram_post(chs):
        for ch in chs:
            d = st[ch]
            a_t, qk_t = [], []
            for m in range(R // LANE):
                rows = slice(m * LANE, (m + 1) * LANE)
                e = jnp.exp(jnp.minimum(d["gc_c"][rows, :] - d["gc_r"][:, rows], 0.0))
                a_t.append(bf(d["kkqk"][rows, rows] * e))
                qk_t.append(bf(d["kkqk"][R + m * LANE:R + (m + 1) * LANE, rows] * e)
                            * mrr_ref[0, rows, rows])
            d["a_t"] = a_t
            d["qk_b"] = lane_tiles(qk_t)
            d["x_b"] = mrr_ref[n_lvl + 1] - lane_tiles(
                [t * mrr_ref[n_lvl, m * LANE:(m + 1) * LANE, m * LANE:(m + 1) * LANE]
                 for m, t in enumerate(a_t)])

    def level_first(chs, lvl):
        for ch in chs:
            d = st[ch]
            d["p"] = mm(d["x_b"], lane_tiles(
                [t * mrr_ref[lvl, m * LANE:(m + 1) * LANE, m * LANE:(m + 1) * LANE]
                 for m, t in enumerate(d["a_t"])]))

    def level_second(chs):
        for ch in chs:
            d = st[ch]
            d["q"] = mm(bf(d["p"]), d["x_b"])

    def level_post(chs):
        for ch in chs:
            d = st[ch]
            d["x_b"] = d["x_b"] - bf(d["q"])

    def solve_issue(chs):
        for ch in chs:
            d = st[ch]
            d["sol"] = mm(d["x_b"], d["rhs"])

    def solve_post(chs):
        for ch in chs:
            d = st[ch]
            d["kq_bd"] = block_diag(bf(jnp.concatenate([d["sol"][:, HEAD_DIM:], d["q_dec"]], axis=0)))

    packs = range(H // HP)
    rec = {}

    def rec_read(c):
        for p in packs:
            s_old = s_scr[p * SR:(p + 1) * SR, :]
            rec[c, p] = dict(s_old=s_old, ks_qs=mm(st[c, p]["kq_bd"], bf(s_old)))

    def rec_inner(c):
        for p in packs:
            d = rec[c, p]
            d["w_b"] = bf(st[c, p]["sol"][:, :HEAD_DIM] - d["ks_qs"][:R])
            d["o"] = mm(st[c, p]["qk_b"], d["w_b"])

    def rec_write(c):
        for p in packs:
            d = rec[c, p]
            s_scr[p * SR:(p + 1) * SR, :] = d["s_old"] * st[c, p]["g_tot"] + lax.dot_general(
                st[c, p]["kdec_bd"], d["w_b"], (((0,), (0,)), ((), ())), preferred_element_type=F32)

    def rec_out(c):
        r0 = c * C
        for p in packs:
            d = rec[c, p]
            o_p = d["ks_qs"][R:] + d["o"]
            for i in range(HP):
                h = p * HP + i
                oh = o_p[i * C:(i + 1) * C, :]
                oh = oh * lax.rsqrt(jnp.mean(oh * oh, axis=-1, keepdims=True) + RMS_EPS) * nw_ref[...]
                zz = z_ref[0, r0:r0 + C, h * HEAD_DIM:(h + 1) * HEAD_DIM]
                o_ref[0, r0:r0 + C, h * HEAD_DIM:(h + 1) * HEAD_DIM] = (oh * _silu(zz)).astype(o_ref.dtype)

    group = max(NCH // 2, 1) * len(packs)
    groups = [chains[g:g + group] for g in range(0, len(chains), group)]
    fillers = []
    for ch in groups[0]:
        prologue(*ch)
    for g, chs in enumerate(groups):
        if g + 1 < len(groups):
            fillers = [functools.partial(prologue, *ch) for ch in groups[g + 1]] + fillers
        fill = lambda: fillers.pop(0)() if fillers else None
        gram_issue(chs)
        gram_post(chs)
        fill()
        for lvl in range(1, n_lvl):
            level_first(chs, lvl)
            level_second(chs)
            level_post(chs)
            fill()
        solve_issue(chs)
        solve_post(chs)
        while fillers:
            fill()
        for c in sorted({c for c, _ in chs}):
            fillers += [functools.partial(rec_read, c), functools.partial(rec_inner, c),
                        lambda c=c: (rec_write(c), rec_out(c))]
    while fillers:
        fill()

    cbuf[0:tail, :] = cbuf[TB:TB + tail, :]
    cnew_ref[0] = cbuf[0:tail, :]

    @pl.when(t == pl.num_programs(1) - 1)
    def _():
        sfin_ref[0] = s_scr[...]


def _delta(proj3, ab3, conv_prev, s0, cw8, gate8, nw, C, NCH, HP):
    bsz, seqlen, _ = proj3.shape
    qkv_dim = cw8.shape[1]
    dn = qkv_dim // 3
    heads = dn // HEAD_DIM
    tb = NCH * C
    kern = functools.partial(_delta_kernel, C=C, NCH=NCH, HP=HP, H=heads)
    masks = _delta_masks(C, HP)
    const = lambda a: pl.BlockSpec(a.shape, lambda b, t: (0,) * a.ndim)
    return pl.pallas_call(
        kern,
        grid=(bsz, seqlen // tb),
        in_specs=[
            pl.BlockSpec((1, tb, qkv_dim), lambda b, t: (b, t, 0)),
            pl.BlockSpec((1, tb, dn), lambda b, t: (b, t, qkv_dim // dn)),
            pl.BlockSpec((1, tb, LANE), lambda b, t: (b, t, 0)),
            pl.BlockSpec((1, SUBLANE, qkv_dim), lambda b, t: (b, 0, 0)),
            pl.BlockSpec((1, dn, HEAD_DIM), lambda b, t: (b, 0, 0)),
            pl.BlockSpec((SUBLANE, qkv_dim), lambda b, t: (0, 0)),
            pl.BlockSpec((SUBLANE, LANE), lambda b, t: (0, 0)),
            pl.BlockSpec((1, HEAD_DIM), lambda b, t: (0, 0)),
            const(masks),
        ],
        out_specs=[
            pl.BlockSpec((1, tb, dn), lambda b, t: (b, t, 0)),
            pl.BlockSpec((1, SUBLANE, qkv_dim), lambda b, t: (b, 0, 0)),
            pl.BlockSpec((1, dn, HEAD_DIM), lambda b, t: (b, 0, 0)),
        ],
        out_shape=[
            jax.ShapeDtypeStruct((bsz, seqlen, dn), BF16),
            jax.ShapeDtypeStruct((bsz, SUBLANE, qkv_dim), F32),
            jax.ShapeDtypeStruct((bsz, dn, HEAD_DIM), F32),
        ],
        scratch_shapes=[pltpu.VMEM((SUBLANE + tb, qkv_dim), F32), pltpu.VMEM((dn, HEAD_DIM), F32)],
        compiler_params=pltpu.CompilerParams(
            dimension_semantics=("arbitrary", "arbitrary"), vmem_limit_bytes=VMEM_LIMIT),
        name="delta_rule",
    )(proj3, proj3, ab3, conv_prev, s0, cw8, gate8, nw, masks)


def _s5_disc_kernel(are_ref, aim_ref, logdt_ref, lre_ref, lim_ref, fre_ref, fim_ref):
    a_re = are_ref[...]
    a_im = aim_ref[...]
    dt = jnp.exp(logdt_ref[...])
    mag = jnp.exp(a_re * dt)
    lam_re = mag * jnp.cos(a_im * dt)
    lam_im = mag * jnp.sin(a_im * dt)
    den = a_re * a_re + a_im * a_im
    nr = lam_re - 1.0
    lre_ref[...] = lam_re
    lim_ref[...] = lam_im
    fre_ref[...] = (nr * a_re + lam_im * a_im) / den
    fim_ref[...] = (lam_im * a_re - nr * a_im) / den


def _s5_disc(a_re, a_im, log_dt):
    shp = jax.ShapeDtypeStruct(a_re.shape, F32)
    return pl.pallas_call(_s5_disc_kernel, out_shape=[shp] * 4, name="s5_disc")(
        a_re, a_im, log_dt.reshape(-1, 1))


def _s5_kernel(u_ref, x0_ref, bb_ref, cc_ref, lam_ref, d_ref, wg_ref, bg_ref,
               y_ref, xfin_ref, utm, xs, ytm, st, *, B, TB, KT, SW):
    t = pl.program_id(0)
    kw = 2 * SW
    cw = u_ref.shape[2] // KT

    @pl.when(t == 0)
    def _():
        st[...] = x0_ref[...]

    nl = u_ref.shape[2] // LANE
    for b in range(B):
        for j in range(nl):
            utm[j, pl.ds(b, TB, stride=B), :] = u_ref[b, :, j * LANE:(j + 1) * LANE]
    u_t = jnp.concatenate([utm[j] for j in range(nl)], axis=1)
    ub = u_t.astype(BF16)

    def input_map(kt):
        xs[:, kt * kw:(kt + 1) * kw] = jnp.dot(ub[:, kt * cw:(kt + 1) * cw], bb_ref[kt],
                                               preferred_element_type=F32)

    def recurrence(kt):
        half = SW // 2
        for hf in range(2):
            c_re = kt * kw + hf * half
            c_im = c_re + SW
            l_re = jnp.broadcast_to(lam_ref[kt:kt + 1, hf * half:(hf + 1) * half], (B, half))
            l_im = jnp.broadcast_to(lam_ref[KT + kt:KT + kt + 1, hf * half:(hf + 1) * half], (B, half))
            x_re = st[:, c_re:c_re + half]
            x_im = st[:, c_im:c_im + half]
            for tt in range(TB):
                rows = slice(tt * B, (tt + 1) * B)
                n_re = l_re * x_re - l_im * x_im + xs[rows, c_re:c_re + half]
                n_im = l_re * x_im + l_im * x_re + xs[rows, c_im:c_im + half]
                xs[rows, c_re:c_re + half] = n_re
                xs[rows, c_im:c_im + half] = n_im
                x_re, x_im = n_re, n_im
            st[:, c_re:c_re + half] = x_re
            st[:, c_im:c_im + half] = x_im

    def output_map(kt):
        return jnp.dot(xs[:, kt * kw:(kt + 1) * kw].astype(BF16), cc_ref[kt],
                       preferred_element_type=F32)

    input_map(0)
    ys = []
    for kt in range(KT):
        if kt + 1 < KT:
            input_map(kt + 1)
        recurrence(kt)
        cols = slice(kt * cw, (kt + 1) * cw)
        y = output_map(kt) + d_ref[:, cols] * u_t[:, cols]
        ys.append(0.5 * y * (1.0 + jnp.tanh(math.sqrt(2.0 / math.pi) * (y + 0.044715 * (y * y * y)))))
    y = jnp.concatenate(ys, axis=1)
    y = y * _sigmoid(jnp.dot(y.astype(BF16), wg_ref[...], preferred_element_type=F32) + bg_ref[...])
    for j in range(nl):
        ytm[j] = y[:, j * LANE:(j + 1) * LANE]
    for b in range(B):
        for j in range(nl):
            y_ref[b, :, j * LANE:(j + 1) * LANE] = ytm[j, pl.ds(b, TB, stride=B), :].astype(y_ref.dtype)

    @pl.when(t == pl.num_programs(0) - 1)
    def _():
        xfin_ref[...] = st[...]


def _s5(proj3, x0, bb, cc, lam, d_row, w_glu, b_glu, TB):
    bsz, seqlen, n = proj3.shape
    width = d_row.shape[1]
    kt, cw, kw = bb.shape
    sdim = x0.shape[1]
    const = lambda shape: pl.BlockSpec(shape, lambda t: (0,) * len(shape))
    kern = functools.partial(_s5_kernel, B=bsz, TB=TB, KT=kt, SW=kw // 2)
    return pl.pallas_call(
        kern,
        grid=(seqlen // TB,),
        in_specs=[
            pl.BlockSpec((bsz, TB, width), lambda t: (0, t, n // width - 1)),
            const((bsz, sdim)), const((kt, cw, kw)), const((kt, kw, cw)), const(lam.shape),
            const((1, width)), const((width, width)), const((1, width)),
        ],
        out_specs=[pl.BlockSpec((bsz, TB, width), lambda t: (0, t, 0)), const((bsz, sdim))],
        out_shape=[jax.ShapeDtypeStruct((bsz, seqlen, width), BF16),
                   jax.ShapeDtypeStruct((bsz, sdim), F32)],
        scratch_shapes=[pltpu.VMEM((width // LANE, bsz * TB, LANE), F32),
                        pltpu.VMEM((bsz * TB, sdim), F32),
                        pltpu.VMEM((width // LANE, bsz * TB, LANE), F32),
                        pltpu.VMEM((bsz, sdim), F32)],
        compiler_params=pltpu.CompilerParams(
            dimension_semantics=("arbitrary",), vmem_limit_bytes=VMEM_LIMIT),
        name="s5",
    )(proj3, x0, bb, cc, lam, d_row, w_glu, b_glu)


def _out_ln_kernel(x_ref, o_ref, y5_ref, lng_ref, lnb_ref, wout_ref, g1_ref, b1_ref, x1_ref,
                   *, alpha):
    dn = o_ref.shape[1]
    h = _layer_norm(x_ref[...], lng_ref[...], lnb_ref[...])
    mix = (jnp.dot(o_ref[...], wout_ref[0:dn, :], preferred_element_type=F32)
           + jnp.dot(y5_ref[...], wout_ref[dn:, :], preferred_element_type=F32))
    x1_ref[...] = _layer_norm(alpha * h + mix, g1_ref[...], b1_ref[...])


def _out_ln(x2, o2, y52, ln_g, ln_b, w_out, g1, b1, alpha, tm):
    rows, d = x2.shape
    row = lambda w: pl.BlockSpec((tm, w), lambda i: (i, 0))
    vec = pl.BlockSpec((1, d), lambda i: (0, 0))
    return pl.pallas_call(
        functools.partial(_out_ln_kernel, alpha=alpha),
        grid=(rows // tm,),
        in_specs=[row(d), row(o2.shape[1]), row(y52.shape[1]), vec, vec,
                  pl.BlockSpec(w_out.shape, lambda i: (0, 0)), vec, vec],
        out_specs=row(d),
        out_shape=jax.ShapeDtypeStruct((rows, d), F32),
        compiler_params=pltpu.CompilerParams(
            dimension_semantics=("arbitrary",), vmem_limit_bytes=VMEM_LIMIT),
        name="out_ln",
    )(x2, o2, y52, ln_g, ln_b, w_out, g1, b1)


def _ffn_kernel(x1_ref, wg_ref, wu_ref, wd_ref, g2_ref, b2_ref, y_ref, x1b, *, alpha):
    j = pl.program_id(1)

    @pl.when(j == 0)
    def _():
        x1 = x1_ref[...]
        x1b[...] = x1.astype(BF16)
        y_ref[...] = alpha * x1

    xb = x1b[...]
    gate = jnp.dot(xb, wg_ref[...], preferred_element_type=F32)
    up = jnp.dot(xb, wu_ref[...], preferred_element_type=F32)
    hb = (_silu(gate) * up).astype(BF16)
    th = wd_ref.shape[0]
    for c in range(0, y_ref.shape[1], th):
        y_ref[:, c:c + th] += jnp.dot(hb, wd_ref[:, c:c + th], preferred_element_type=F32)

    @pl.when(j == pl.num_programs(1) - 1)
    def _():
        y_ref[...] = _layer_norm(y_ref[...], g2_ref[...], b2_ref[...])


def _ffn(x1, w_gate, w_up, w_down, g2, b2, alpha, tm, th):
    rows, d = x1.shape
    fh = w_gate.shape[1]
    vec = pl.BlockSpec((1, d), lambda i, j: (0, 0))
    return pl.pallas_call(
        functools.partial(_ffn_kernel, alpha=alpha),
        grid=(rows // tm, fh // th),
        in_specs=[
            pl.BlockSpec((tm, d), lambda i, j: (i, 0)),
            pl.BlockSpec((d, th), lambda i, j: (0, j)),
            pl.BlockSpec((d, th), lambda i, j: (0, j)),
            pl.BlockSpec((th, d), lambda i, j: (j, 0)),
            vec, vec,
        ],
        out_specs=pl.BlockSpec((tm, d), lambda i, j: (i, 0)),
        out_shape=jax.ShapeDtypeStruct((rows, d), F32),
        scratch_shapes=[pltpu.VMEM((tm, d), BF16)],
        compiler_params=pltpu.CompilerParams(
            dimension_semantics=("arbitrary", "arbitrary"), vmem_limit_bytes=FFN_VMEM_LIMIT),
        name="ffn",
    )(x1, w_gate, w_up, w_down, g2, b2)


def _largest_divisor(n, cap, mult):
    best = mult
    for c in range(mult, cap + 1, mult):
        if n % c == 0:
            best = c
    return best


def _layer(x3, conv_prev8, s0, x0, prm, chunk, chunks_per_step, heads_per_pack, s5_tb):
    bsz, seqlen, d = x3.shape
    rows = bsz * seqlen
    x2 = x3.reshape(rows, d)
    tm = _largest_divisor(rows, 1024, 16)
    proj, ab = _ln_inproj(x2, prm["ln_in_g"], prm["ln_in_b"], prm["w_qkvz"], prm["w_u"], prm["w_ab"], tm)
    proj3 = proj.reshape(bsz, seqlen, -1)
    o, conv_new, s_new = _delta(proj3, ab.reshape(bsz, seqlen, LANE), conv_prev8, s0,
                                prm["cw8"], prm["gate8"], prm["nw"], chunk, chunks_per_step,
                                heads_per_pack)
    y5, x_new = _s5(proj3, x0, prm["bb"], prm["cc"], prm["lam"], prm["s5_d"], prm["w_glu"],
                    prm["b_glu"], s5_tb)
    x1 = _out_ln(x2, o.reshape(rows, -1), y5.reshape(rows, -1), prm["ln_in_g"], prm["ln_in_b"],
                 prm["w_out"], prm["ln1_g"], prm["ln1_b"], prm["alpha"],
                 _largest_divisor(rows, 512, 16))
    y = _ffn(x1, prm["w_gate"], prm["w_up"], prm["w_down"], prm["ln2_g"], prm["ln2_b"],
             prm["alpha"], tm, _largest_divisor(prm["w_gate"].shape[1], 512, LANE))
    return y.reshape(bsz, seqlen, d), conv_new, s_new, x_new


def kernel(x_prompt, x_sample, state_conv_qkv, state_delta, state_s5_re, state_s5_im, meta_tokens, ln_in_g, ln_in_b, w_in, conv_w, dn_a_log, dn_dt_bias, dn_norm_w, s5_a_re, s5_a_im, s5_log_dt, s5_b_re, s5_b_im, s5_c_re, s5_c_im, s5_d, s5_w_glu, s5_b_glu, w_out, ln1_g, ln1_b, ffn_w_gate, ffn_w_up, ffn_w_down, ln2_g, ln2_b):
    depth = w_in.shape[0]
    assert depth == 1, "single-layer trunk only"
    bp, seq, d = x_prompt.shape
    bs, dec_seq, _ = x_sample.shape
    n_meta = meta_tokens.shape[0]
    qkv_dim = conv_w.shape[-1]
    dn = qkv_dim // 3
    heads = dn // HEAD_DIM
    groups, sdim = s5_a_re.shape[1:]
    width = s5_d.shape[-1]
    assert n_meta == dec_seq and seq % CHUNK == 0 and width == groups * S5_GROUP_CH
    alpha = (2.0 * depth) ** 0.25
    row = lambda v: v.reshape(1, -1).astype(F32)

    w = w_in[0]
    n_gate = 2 * heads
    w_qkvz = w[:, :qkv_dim + dn].astype(BF16)
    w_u = w[:, qkv_dim + dn + n_gate:].astype(BF16)
    w_ab = jnp.pad(w[:, qkv_dim + dn:qkv_dim + dn + n_gate], ((0, 0), (0, LANE - n_gate))).astype(BF16)
    gate8 = jnp.zeros((SUBLANE, LANE), F32)
    gate8 = gate8.at[0, :heads].set(dn_a_log[0]).at[1, :heads].set(dn_dt_bias[0])
    cw8 = jnp.pad(conv_w[0].astype(F32), ((0, SUBLANE - CONV_W), (0, 0)))

    lam_re, lam_im, f_re, f_im = _s5_disc(s5_a_re[0], s5_a_im[0], s5_log_dt[0])
    bb_re = f_re[..., None] * s5_b_re[0] - f_im[..., None] * s5_b_im[0]
    bb_im = f_re[..., None] * s5_b_im[0] + f_im[..., None] * s5_b_re[0]
    gk = GROUPS_PER_KTILE
    kt = groups // gk
    eye = jnp.eye(gk, dtype=F32)
    bb5 = jnp.stack([bb_re, bb_im]).reshape(2, kt, gk, sdim, S5_GROUP_CH)
    bb = jnp.einsum("rkgph,gG->kGhrgp", bb5, eye).reshape(kt, gk * S5_GROUP_CH, 2 * gk * sdim).astype(BF16)
    cc5 = jnp.stack([s5_c_re[0], -s5_c_im[0]]).reshape(2, kt, gk, S5_GROUP_CH, sdim)
    cc = jnp.einsum("rkghp,gG->krgpGh", cc5, eye).reshape(kt, 2 * gk * sdim, gk * S5_GROUP_CH).astype(BF16)
    lam = jnp.concatenate([lam_re.reshape(kt, gk * sdim), lam_im.reshape(kt, gk * sdim)], axis=0)

    prm = dict(
        ln_in_g=row(ln_in_g), ln_in_b=row(ln_in_b), w_qkvz=w_qkvz, w_u=w_u, w_ab=w_ab, cw8=cw8, gate8=gate8,
        nw=row(dn_norm_w[0]), bb=bb, cc=cc, lam=lam, s5_d=row(s5_d[0]), w_glu=s5_w_glu[0].astype(BF16),
        b_glu=row(s5_b_glu[0]), w_out=w_out[0].astype(BF16), ln1_g=row(ln1_g[0]), ln1_b=row(ln1_b[0]),
        w_gate=ffn_w_gate[0].astype(BF16), w_up=ffn_w_up[0].astype(BF16),
        w_down=ffn_w_down[0].astype(BF16), ln2_g=row(ln2_g[0]), ln2_b=row(ln2_b[0]), alpha=alpha)

    def s5_pack(re, im):
        b = re.shape[0]
        return jnp.stack([re.reshape(b, kt, gk, sdim), im.reshape(b, kt, gk, sdim)], axis=2).reshape(b, -1)

    def s5_unpack(x):
        b = x.shape[0]
        x = x.reshape(b, kt, 2, gk, sdim)
        return x[:, :, 0].reshape(b, groups, sdim), x[:, :, 1].reshape(b, groups, sdim)

    nb = 2 * SUBLANE
    assert bs + 1 <= nb
    pad_b = lambda a: jnp.pad(a.astype(F32), ((0, nb - a.shape[0]),) + ((0, 0),) * (a.ndim - 1))
    x_small = pad_b(jnp.concatenate([x_sample, meta_tokens[None].astype(x_sample.dtype)], axis=0))
    conv_small = pad_b(jnp.pad(state_conv_qkv[0], ((0, 0), (SUBLANE - CONV_W + 1, 0), (0, 0))))
    s0_small = pad_b(state_delta[0].reshape(bs, dn, HEAD_DIM))
    x0_small = pad_b(s5_pack(state_s5_re[0], state_s5_im[0]))
    y_s, conv_s, s_s, xf_s = _layer(x_small, conv_small, s0_small, x0_small, prm,
                                    dec_seq, 1, heads, dec_seq)

    bc = lambda a: jnp.broadcast_to(a[bs:bs + 1], (bp,) + a.shape[1:])
    y_p, conv_p, s_p, xf_p = _layer(x_prompt, bc(conv_s), bc(s_s), bc(xf_s), prm,
                                    CHUNK, DELTA_CHUNKS_PER_STEP, 256 // CHUNK, CHUNK)

    re_p, im_p = s5_unpack(xf_p)
    re_s, im_s = s5_unpack(xf_s[:bs])
    tail = SUBLANE - CONV_W + 1
    return (y_p.astype(x_prompt.dtype), y_s[:bs].astype(x_sample.dtype),
            conv_p[None, :, tail:].astype(state_conv_qkv.dtype),
            s_p.reshape(bp, heads, HEAD_DIM, HEAD_DIM)[None].astype(state_delta.dtype),
            re_p[None].astype(state_s5_re.dtype), im_p[None].astype(state_s5_im.dtype),
            conv_s[None, :bs, tail:].astype(state_conv_qkv.dtype),
            s_s[:bs].reshape(bs, heads, HEAD_DIM, HEAD_DIM)[None].astype(state_delta.dtype),
            re_s[None].astype(state_s5_re.dtype), im_s[None].astype(state_s5_im.dtype))
```

```python
import functools
import math

import jax
import jax.numpy as jnp
import numpy as np
from jax import lax
from jax.experimental import pallas as pl
from jax.experimental.pallas import tpu as pltpu

LN_EPS = 1e-5
RMS_EPS = 1e-6
CHUNK = 64
HEAD_DIM = 128
DELTA_CHUNKS_PER_STEP = 4
CONV_W = 4
S5_GROUP_CH = 16
GROUPS_PER_KTILE = 16
LANE = 128
SUBLANE = 8
VMEM_LIMIT = 56 * 1024 * 1024
FFN_VMEM_LIMIT = 60 * 1024 * 1024

F32 = jnp.float32
BF16 = jnp.bfloat16


def _layer_norm(x, g, b):
    mu = jnp.mean(x, axis=-1, keepdims=True)
    xc = x - mu
    var = jnp.mean(xc * xc, axis=-1, keepdims=True)
    return xc * lax.rsqrt(var + LN_EPS) * g + b


def _sigmoid(x):
    return 0.5 + 0.5 * jnp.tanh(0.5 * x)


def _silu(x):
    h = 0.5 * x
    return h + h * jnp.tanh(h)


def _softplus(x):
    return jnp.maximum(x, 0.0) + jnp.log1p(jnp.exp(-jnp.abs(x)))


def _ln_inproj_kernel(x_ref, g_ref, b_ref, w_ref, wu_ref, wab_ref, proj_ref, ab_ref, h_scr):
    j = pl.program_id(1)
    last = pl.num_programs(1) - 1

    @pl.when(j == 0)
    def _():
        h = _layer_norm(x_ref[...], g_ref[...], b_ref[...]).astype(BF16)
        h_scr[...] = h
        ab_ref[...] = jnp.dot(h, wab_ref[...], preferred_element_type=F32)

    @pl.when(j < last)
    def _():
        proj_ref[...] = jnp.dot(h_scr[...], w_ref[...], preferred_element_type=F32)

    @pl.when(j == last)
    def _():
        proj_ref[...] = jnp.dot(h_scr[...], wu_ref[...], preferred_element_type=F32)


def _ln_inproj(x2, ln_g, ln_b, w_qkvz, w_u, w_ab, tm):
    rows, d = x2.shape
    tn = w_u.shape[1]
    na = w_qkvz.shape[1] // tn
    assert w_qkvz.shape[1] == na * tn
    return pl.pallas_call(
        _ln_inproj_kernel,
        grid=(rows // tm, na + 1),
        in_specs=[
            pl.BlockSpec((tm, d), lambda i, j: (i, 0)),
            pl.BlockSpec((1, d), lambda i, j: (0, 0)),
            pl.BlockSpec((1, d), lambda i, j: (0, 0)),
            pl.BlockSpec((d, tn), lambda i, j: (0, jnp.minimum(j, na - 1))),
            pl.BlockSpec((d, tn), lambda i, j: (0, 0)),
            pl.BlockSpec((d, LANE), lambda i, j: (0, 0)),
        ],
        out_specs=[
            pl.BlockSpec((tm, tn), lambda i, j: (i, j)),
            pl.BlockSpec((tm, LANE), lambda i, j: (i, 0)),
        ],
        out_shape=[jax.ShapeDtypeStruct((rows, (na + 1) * tn), F32),
                   jax.ShapeDtypeStruct((rows, LANE), F32)],
        scratch_shapes=[pltpu.VMEM((tm, d), BF16)],
        compiler_params=pltpu.CompilerParams(
            dimension_semantics=("arbitrary", "arbitrary"), vmem_limit_bytes=VMEM_LIMIT),
        name="ln_inproj",
    )(x2, ln_g, ln_b, w_qkvz, w_u, w_ab)


def _delta_masks(C, HP):
    R = HP * C
    ri = np.arange(R)[:, None]
    ci = np.arange(R)[None, :]
    same = (ri // C) == (ci // C)
    rr = [same & (ri >= ci)]
    s = 2
    while s < C:
        rr.append(((ri // (2 * s)) == (ci // (2 * s))) & ((ri // s) % 2 == 1) & ((ci // s) % 2 == 0))
        s *= 2
    rr.append(((ri // 2) == (ci // 2)) & (ri % 2 == 1) & (ci % 2 == 0))
    rr.append(ri == ci)
    return jnp.asarray(np.stack(rr), BF16)


def _delta_kernel(qkv_ref, z_ref, ab_ref, cprev_ref, s0_ref, cw_ref, gate_ref, nw_ref,
                  mrr_ref,
                  o_ref, cnew_ref, sfin_ref, cbuf, s_scr, *, C, NCH, HP, H):
    t = pl.program_id(1)
    dn = H * HEAD_DIM
    R = HP * C
    SR = HP * HEAD_DIM
    TB = NCH * C
    tail = SUBLANE

    @pl.when(t == 0)
    def _():
        cbuf[0:tail, :] = cprev_ref[0]
        s_scr[...] = s0_ref[0]

    cbuf[tail:tail + TB, :] = qkv_ref[0]

    def conv_cols(r0, c0):
        ext = cbuf[r0:r0 + tail + C, c0:c0 + HEAD_DIM]
        acc = ext[tail:, :] * cw_ref[CONV_W - 1:CONV_W, c0:c0 + HEAD_DIM]
        for i in range(CONV_W - 1):
            shifted = pltpu.roll(ext, CONV_W - 1 - i, 0)[tail:, :]
            acc = acc + shifted * cw_ref[i:i + 1, c0:c0 + HEAD_DIM]
        return _silu(acc)

    def l2n(x):
        return x * lax.rsqrt(jnp.sum(x * x, axis=-1, keepdims=True) + RMS_EPS)

    def block_diag(x):
        zero = jnp.zeros((C, HEAD_DIM), BF16)
        return jnp.concatenate(
            [jnp.concatenate([x[r * C:(r + 1) * C, :] if r % HP == b else zero for b in range(HP)], axis=1)
             for r in range(x.shape[0] // C)], axis=0)

    bf = lambda x: x.astype(BF16)
    mm = lambda a, b: jnp.dot(a, b, preferred_element_type=F32)
    rowc = lax.broadcasted_iota(jnp.int32, (C, LANE), 0)
    reps = LANE // C
    lane_rep = lax.broadcasted_iota(jnp.int32, (1, LANE), 1) // C

    chains = [(c, p) for c in range(NCH) for p in range(H // HP)]
    gates = {}
    for c in range(NCH):
        r0 = c * C
        ab = ab_ref[0, r0:r0 + C, :]
        g_all = -jnp.exp(gate_ref[0:1, :]) * _softplus(ab + gate_ref[1:2, :])
        beta_all = _sigmoid(ab)
        gcum_all = g_all
        s = 1
        while s < C:
            gcum_all = gcum_all + jnp.where(rowc >= s, pltpu.roll(gcum_all, s, 0), 0.0)
            s *= 2
        gates[c] = (beta_all, gcum_all, gcum_all[C - 1:C, :],
                    jnp.concatenate([gcum_all] * reps, axis=0).T)

    st = {}

    def prologue(c, p):
        r0 = c * C
        beta_all, gcum_all, glast_all, g_t = gates[c]
        heads = range(p * HP, (p + 1) * HP)
        qs = jnp.concatenate(
            [l2n(conv_cols(r0, h * HEAD_DIM)) * (HEAD_DIM ** -0.5) for h in heads], axis=0)
        ks = jnp.concatenate([l2n(conv_cols(r0, dn + h * HEAD_DIM)) for h in heads], axis=0)
        vs = jnp.concatenate([conv_cols(r0, 2 * dn + h * HEAD_DIM) for h in heads], axis=0)
        beta_c = jnp.concatenate([beta_all[:, H + h:H + h + 1] for h in heads], axis=0)
        gc_c = jnp.concatenate([gcum_all[:, h:h + 1] for h in heads], axis=0)
        gl_c = jnp.concatenate(
            [jnp.broadcast_to(glast_all[:, h:h + 1], (C, 1)) for h in heads], axis=0)
        row_blocks = []
        for m in range(R // LANE):
            blk = g_t[p * HP + m * reps:p * HP + m * reps + 1, :]
            for r in range(1, reps):
                hr = p * HP + m * reps + r
                blk = jnp.where(lane_rep == r, g_t[hr:hr + 1, :], blk)
            row_blocks.append(blk)
        gc_r = jnp.concatenate(row_blocks, axis=1)
        kb = ks * beta_c
        gexp = jnp.exp(gc_c)
        g_tot = jnp.concatenate(
            [jnp.broadcast_to(jnp.exp(glast_all[:, h:h + 1]), (HEAD_DIM, 1)) for h in heads], axis=0)
        st[c, p] = dict(
            gc_c=gc_c, gc_r=gc_r, lhs=bf(jnp.concatenate([kb, qs], axis=0)), ks_b=bf(ks),
            rhs=bf(jnp.concatenate([vs * beta_c, kb * gexp], axis=1)), q_dec=qs * gexp,
            kdec_bd=block_diag(bf(ks * jnp.exp(gl_c - gc_c))), g_tot=g_tot)

    def gram_issue(chs):
        for ch in chs:
            d = st[ch]
            d["kkqk"] = lax.dot_general(d["lhs"], d["ks_b"], (((1,), (1,)), ((), ())),
                                        preferred_element_type=F32)

    n_lvl = mrr_ref.shape[0] - 2

    def lane_tiles(tiles):
        zero = jnp.zeros((LANE, LANE), BF16)
        return jnp.concatenate(
            [jnp.concatenate([t if a == b else zero for b in range(len(tiles))], axis=1)
             for a, t in enumerate(tiles)], axis=0)

    def gram_post(chs):
        for ch in chs:
            d = st[ch]
            a_t, qk_t = [], []
            for m in range(R // LANE):
                rows = slice(m * LANE, (m + 1) * LANE)
                e = jnp.exp(jnp.minimum(d["gc_c"][rows, :] - d["gc_r"][:, rows], 0.0))
                a_t.append(bf(d["kkqk"][rows, rows] * e))
                qk_t.append(bf(d["kkqk"][R + m * LANE:R + (m + 1) * LANE, rows] * e)
                            * mrr_ref[0, rows, rows])
            d["a_t"] = a_t
            d["qk_b"] = lane_tiles(qk_t)
            d["x_b"] = mrr_ref[n_lvl + 1] - lane_tiles(
                [t * mrr_ref[n_lvl, m * LANE:(m + 1) * LANE, m * LANE:(m + 1) * LANE]
                 for m, t in enumerate(a_t)])

    def level_first(chs, lvl):
        for ch in chs:
            d = st[ch]
            d["p"] = mm(d["x_b"], lane_tiles(
                [t * mrr_ref[lvl, m * LANE:(m + 1) * LANE, m * LANE:(m + 1) * LANE]
                 for m, t in enumerate(d["a_t"])]))

    def level_second(chs):
        for ch in chs:
            d = st[ch]
            d["q"] = mm(bf(d["p"]), d["x_b"])

    def level_post(chs):
        for ch in chs:
            d = st[ch]
            d["x_b"] = d["x_b"] - bf(d["q"])

    def solve_issue(chs):
        for ch in chs:
            d = st[ch]
            d["sol"] = mm(d["x_b"], d["rhs"])

    def solve_post(chs):
        for ch in chs:
            d = st[ch]
            d["kq_bd"] = block_diag(bf(jnp.concatenate([d["sol"][:, HEAD_DIM:], d["q_dec"]], axis=0)))

    packs = range(H // HP)
    rec = {}

    def rec_read(c):
        for p in packs:
            s_old = s_scr[p * SR:(p + 1) * SR, :]
            rec[c, p] = dict(s_old=s_old, ks_qs=mm(st[c, p]["kq_bd"], bf(s_old)))

    def rec_inner(c):
        for p in packs:
            d = rec[c, p]
            d["w_b"] = bf(st[c, p]["sol"][:, :HEAD_DIM] - d["ks_qs"][:R])
            d["o"] = mm(st[c, p]["qk_b"], d["w_b"])

    def rec_write(c):
        for p in packs:
            d = rec[c, p]
            s_scr[p * SR:(p + 1) * SR, :] = d["s_old"] * st[c, p]["g_tot"] + lax.dot_general(
                st[c, p]["kdec_bd"], d["w_b"], (((0,), (0,)), ((), ())), preferred_element_type=F32)

    def rec_out(c):
        r0 = c * C
        for p in packs:
            d = rec[c, p]
            o_p = d["ks_qs"][R:] + d["o"]
            for i in range(HP):
                h = p * HP + i
                oh = o_p[i * C:(i + 1) * C, :]
                oh = oh * lax.rsqrt(jnp.mean(oh * oh, axis=-1, keepdims=True) + RMS_EPS) * nw_ref[...]
                zz = z_ref[0, r0:r0 + C, h * HEAD_DIM:(h + 1) * HEAD_DIM]
                o_ref[0, r0:r0 + C, h * HEAD_DIM:(h + 1) * HEAD_DIM] = (oh * _silu(zz)).astype(o_ref.dtype)

    group = max(NCH // 2, 1) * len(packs)
    groups = [chains[g:g + group] for g in range(0, len(chains), group)]
    fillers = []
    for ch in groups[0]:
        prologue(*ch)
    for g, chs in enumerate(groups):
        if g + 1 < len(groups):
            fillers = [functools.partial(prologue, *ch) for ch in groups[g + 1]] + fillers
        fill = lambda: fillers.pop(0)() if fillers else None
        gram_issue(chs)
        gram_post(chs)
        fill()
        for lvl in range(1, n_lvl):
            level_first(chs, lvl)
            level_second(chs)
            level_post(chs)
            fill()
        solve_issue(chs)
        solve_post(chs)
        while fillers:
            fill()
        for c in sorted({c for c, _ in chs}):
            fillers += [functools.partial(rec_read, c), functools.partial(rec_inner, c),
                        lambda c=c: (rec_write(c), rec_out(c))]
    while fillers:
        fill()

    cbuf[0:tail, :] = cbuf[TB:TB + tail, :]
    cnew_ref[0] = cbuf[0:tail, :]

    @pl.when(t == pl.num_programs(1) - 1)
    def _():
        sfin_ref[0] = s_scr[...]


def _delta(proj3, ab3, conv_prev, s0, cw8, gate8, nw, C, NCH, HP):
    bsz, seqlen, _ = proj3.shape
    qkv_dim = cw8.shape[1]
    dn = qkv_dim // 3
    heads = dn // HEAD_DIM
    tb = NCH * C
    kern = functools.partial(_delta_kernel, C=C, NCH=NCH, HP=HP, H=heads)
    masks = _delta_masks(C, HP)
    const = lambda a: pl.BlockSpec(a.shape, lambda b, t: (0,) * a.ndim)
    return pl.pallas_call(
        kern,
        grid=(bsz, seqlen // tb),
        in_specs=[
            pl.BlockSpec((1, tb, qkv_dim), lambda b, t: (b, t, 0)),
            pl.BlockSpec((1, tb, dn), lambda b, t: (b, t, qkv_dim // dn)),
            pl.BlockSpec((1, tb, LANE), lambda b, t: (b, t, 0)),
            pl.BlockSpec((1, SUBLANE, qkv_dim), lambda b, t: (b, 0, 0)),
            pl.BlockSpec((1, dn, HEAD_DIM), lambda b, t: (b, 0, 0)),
            pl.BlockSpec((SUBLANE, qkv_dim), lambda b, t: (0, 0)),
            pl.BlockSpec((SUBLANE, LANE), lambda b, t: (0, 0)),
            pl.BlockSpec((1, HEAD_DIM), lambda b, t: (0, 0)),
            const(masks),
        ],
        out_specs=[
            pl.BlockSpec((1, tb, dn), lambda b, t: (b, t, 0)),
            pl.BlockSpec((1, SUBLANE, qkv_dim), lambda b, t: (b, 0, 0)),
            pl.BlockSpec((1, dn, HEAD_DIM), lambda b, t: (b, 0, 0)),
        ],
        out_shape=[
            jax.ShapeDtypeStruct((bsz, seqlen, dn), BF16),
            jax.ShapeDtypeStruct((bsz, SUBLANE, qkv_dim), F32),
            jax.ShapeDtypeStruct((bsz, dn, HEAD_DIM), F32),
        ],
        scratch_shapes=[pltpu.VMEM((SUBLANE + tb, qkv_dim), F32), pltpu.VMEM((dn, HEAD_DIM), F32)],
        compiler_params=pltpu.CompilerParams(
            dimension_semantics=("arbitrary", "arbitrary"), vmem_limit_bytes=VMEM_LIMIT),
        name="delta_rule",
    )(proj3, proj3, ab3, conv_prev, s0, cw8, gate8, nw, masks)


def _s5_disc_kernel(are_ref, aim_ref, logdt_ref, lre_ref, lim_ref, fre_ref, fim_ref):
    a_re = are_ref[...]
    a_im = aim_ref[...]
    dt = jnp.exp(logdt_ref[...])
    mag = jnp.exp(a_re * dt)
    lam_re = mag * jnp.cos(a_im * dt)
    lam_im = mag * jnp.sin(a_im * dt)
    den = a_re * a_re + a_im * a_im
    nr = lam_re - 1.0
    lre_ref[...] = lam_re
    lim_ref[...] = lam_im
    fre_ref[...] = (nr * a_re + lam_im * a_im) / den
    fim_ref[...] = (lam_im * a_re - nr * a_im) / den


def _s5_disc(a_re, a_im, log_dt):
    shp = jax.ShapeDtypeStruct(a_re.shape, F32)
    return pl.pallas_call(_s5_disc_kernel, out_shape=[shp] * 4, name="s5_disc")(
        a_re, a_im, log_dt.reshape(-1, 1))


def _s5_kernel(u_ref, x0_ref, bb_ref, cc_ref, lam_ref, d_ref, wg_ref, bg_ref,
               y_ref, xfin_ref, utm, xs, ytm, st, *, B, TB, KT, SW):
    t = pl.program_id(0)
    kw = 2 * SW
    cw = u_ref.shape[2] // KT

    @pl.when(t == 0)
    def _():
        st[...] = x0_ref[...]

    nl = u_ref.shape[2] // LANE
    for b in range(B):
        for j in range(nl):
            utm[j, pl.ds(b, TB, stride=B), :] = u_ref[b, :, j * LANE:(j + 1) * LANE]
    u_t = jnp.concatenate([utm[j] for j in range(nl)], axis=1)
    ub = u_t.astype(BF16)

    def input_map(kt):
        xs[:, kt * kw:(kt + 1) * kw] = jnp.dot(ub[:, kt * cw:(kt + 1) * cw], bb_ref[kt],
                                               preferred_element_type=F32)

    def recurrence(kt):
        half = SW // 2
        for hf in range(2):
            c_re = kt * kw + hf * half
            c_im = c_re + SW
            l_re = jnp.broadcast_to(lam_ref[kt:kt + 1, hf * half:(hf + 1) * half], (B, half))
            l_im = jnp.broadcast_to(lam_ref[KT + kt:KT + kt + 1, hf * half:(hf + 1) * half], (B, half))
            x_re = st[:, c_re:c_re + half]
            x_im = st[:, c_im:c_im + half]
            for tt in range(TB):
                rows = slice(tt * B, (tt + 1) * B)
                n_re = l_re * x_re - l_im * x_im + xs[rows, c_re:c_re + half]
                n_im = l_re * x_im + l_im * x_re + xs[rows, c_im:c_im + half]
                xs[rows, c_re:c_re + half] = n_re
                xs[rows, c_im:c_im + half] = n_im
                x_re, x_im = n_re, n_im
            st[:, c_re:c_re + half] = x_re
            st[:, c_im:c_im + half] = x_im

    def output_map(kt):
        return jnp.dot(xs[:, kt * kw:(kt + 1) * kw].astype(BF16), cc_ref[kt],
                       preferred_element_type=F32)

    input_map(0)
    ys = []
    for kt in range(KT):
        if kt + 1 < KT:
            input_map(kt + 1)
        recurrence(kt)
        cols = slice(kt * cw, (kt + 1) * cw)
        y = output_map(kt) + d_ref[:, cols] * u_t[:, cols]
        ys.append(0.5 * y * (1.0 + jnp.tanh(math.sqrt(2.0 / math.pi) * (y + 0.044715 * (y * y * y)))))
    y = jnp.concatenate(ys, axis=1)
    y = y * _sigmoid(jnp.dot(y.astype(BF16), wg_ref[...], preferred_element_type=F32) + bg_ref[...])
    for j in range(nl):
        ytm[j] = y[:, j * LANE:(j + 1) * LANE]
    for b in range(B):
        for j in range(nl):
            y_ref[b, :, j * LANE:(j + 1) * LANE] = ytm[j, pl.ds(b, TB, stride=B), :].astype(y_ref.dtype)

    @pl.when(t == pl.num_programs(0) - 1)
    def _():
        xfin_ref[...] = st[...]


def _s5(proj3, x0, bb, cc, lam, d_row, w_glu, b_glu, TB):
    bsz, seqlen, n = proj3.shape
    width = d_row.shape[1]
    kt, cw, kw = bb.shape
    sdim = x0.shape[1]
    const = lambda shape: pl.BlockSpec(shape, lambda t: (0,) * len(shape))
    kern = functools.partial(_s5_kernel, B=bsz, TB=TB, KT=kt, SW=kw // 2)
    return pl.pallas_call(
        kern,
        grid=(seqlen // TB,),
        in_specs=[
            pl.BlockSpec((bsz, TB, width), lambda t: (0, t, n // width - 1)),
            const((bsz, sdim)), const((kt, cw, kw)), const((kt, kw, cw)), const(lam.shape),
            const((1, width)), const((width, width)), const((1, width)),
        ],
        out_specs=[pl.BlockSpec((bsz, TB, width), lambda t: (0, t, 0)), const((bsz, sdim))],
        out_shape=[jax.ShapeDtypeStruct((bsz, seqlen, width), BF16),
                   jax.ShapeDtypeStruct((bsz, sdim), F32)],
        scratch_shapes=[pltpu.VMEM((width // LANE, bsz * TB, LANE), F32),
                        pltpu.VMEM((bsz * TB, sdim), F32),
                        pltpu.VMEM((width // LANE, bsz * TB, LANE), F32),
                        pltpu.VMEM((bsz, sdim), F32)],
        compiler_params=pltpu.CompilerParams(
            dimension_semantics=("arbitrary",), vmem_limit_bytes=VMEM_LIMIT),
        name="s5",
    )(proj3, x0, bb, cc, lam, d_row, w_glu, b_glu)


def _out_ln_kernel(x_ref, o_ref, y5_ref, lng_ref, lnb_ref, wout_ref, g1_ref, b1_ref, x1_ref,
                   *, alpha):
    dn = o_ref.shape[1]
    h = _layer_norm(x_ref[...], lng_ref[...], lnb_ref[...])
    mix = (jnp.dot(o_ref[...], wout_ref[0:dn, :], preferred_element_type=F32)
           + jnp.dot(y5_ref[...], wout_ref[dn:, :], preferred_element_type=F32))
    x1_ref[...] = _layer_norm(alpha * h + mix, g1_ref[...], b1_ref[...])


def _out_ln(x2, o2, y52, ln_g, ln_b, w_out, g1, b1, alpha, tm):
    rows, d = x2.shape
    row = lambda w: pl.BlockSpec((tm, w), lambda i: (i, 0))
    vec = pl.BlockSpec((1, d), lambda i: (0, 0))
    return pl.pallas_call(
        functools.partial(_out_ln_kernel, alpha=alpha),
        grid=(rows // tm,),
        in_specs=[row(d), row(o2.shape[1]), row(y52.shape[1]), vec, vec,
                  pl.BlockSpec(w_out.shape, lambda i: (0, 0)), vec, vec],
        out_specs=row(d),
        out_shape=jax.ShapeDtypeStruct((rows, d), F32),
        compiler_params=pltpu.CompilerParams(
            dimension_semantics=("arbitrary",), vmem_limit_bytes=VMEM_LIMIT),
        name="out_ln",
    )(x2, o2, y52, ln_g, ln_b, w_out, g1, b1)


def _ffn_kernel(x1_ref, wg_ref, wu_ref, wd_ref, g2_ref, b2_ref, y_ref, x1b, *, alpha):
    j = pl.program_id(1)

    @pl.when(j == 0)
    def _():
        x1 = x1_ref[...]
        x1b[...] = x1.astype(BF16)
        y_ref[...] = alpha * x1

    xb = x1b[...]
    gate = jnp.dot(xb, wg_ref[...], preferred_element_type=F32)
    up = jnp.dot(xb, wu_ref[...], preferred_element_type=F32)
    hb = (_silu(gate) * up).astype(BF16)
    th = wd_ref.shape[0]
    for c in range(0, y_ref.shape[1], th):
        y_ref[:, c:c + th] += jnp.dot(hb, wd_ref[:, c:c + th], preferred_element_type=F32)

    @pl.when(j == pl.num_programs(1) - 1)
    def _():
        y_ref[...] = _layer_norm(y_ref[...], g2_ref[...], b2_ref[...])


def _ffn(x1, w_gate, w_up, w_down, g2, b2, alpha, tm, th):
    rows, d = x1.shape
    fh = w_gate.shape[1]
    vec = pl.BlockSpec((1, d), lambda i, j: (0, 0))
    return pl.pallas_call(
        functools.partial(_ffn_kernel, alpha=alpha),
        grid=(rows // tm, fh // th),
        in_specs=[
            pl.BlockSpec((tm, d), lambda i, j: (i, 0)),
            pl.BlockSpec((d, th), lambda i, j: (0, j)),
            pl.BlockSpec((d, th), lambda i, j: (0, j)),
            pl.BlockSpec((th, d), lambda i, j: (j, 0)),
            vec, vec,
        ],
        out_specs=pl.BlockSpec((tm, d), lambda i, j: (i, 0)),
        out_shape=jax.ShapeDtypeStruct((rows, d), F32),
        scratch_shapes=[pltpu.VMEM((tm, d), BF16)],
        compiler_params=pltpu.CompilerParams(
            dimension_semantics=("arbitrary", "arbitrary"), vmem_limit_bytes=FFN_VMEM_LIMIT),
        name="ffn",
    )(x1, w_gate, w_up, w_down, g2, b2)


def _largest_divisor(n, cap, mult):
    best = mult
    for c in range(mult, cap + 1, mult):
        if n % c == 0:
            best = c
    return best


def _layer(x3, conv_prev8, s0, x0, prm, chunk, chunks_per_step, heads_per_pack, s5_tb):
    bsz, seqlen, d = x3.shape
    rows = bsz * seqlen
    x2 = x3.reshape(rows, d)
    tm = _largest_divisor(rows, 1024, 16)
    proj, ab = _ln_inproj(x2, prm["ln_in_g"], prm["ln_in_b"], prm["w_qkvz"], prm["w_u"], prm["w_ab"], tm)
    proj3 = proj.reshape(bsz, seqlen, -1)
    o, conv_new, s_new = _delta(proj3, ab.reshape(bsz, seqlen, LANE), conv_prev8, s0,
                                prm["cw8"], prm["gate8"], prm["nw"], chunk, chunks_per_step,
                                heads_per_pack)
    y5, x_new = _s5(proj3, x0, prm["bb"], prm["cc"], prm["lam"], prm["s5_d"], prm["w_glu"],
                    prm["b_glu"], s5_tb)
    x1 = _out_ln(x2, o.reshape(rows, -1), y5.reshape(rows, -1), prm["ln_in_g"], prm["ln_in_b"],
                 prm["w_out"], prm["ln1_g"], prm["ln1_b"], prm["alpha"],
                 _largest_divisor(rows, 512, 16))
    y = _ffn(x1, prm["w_gate"], prm["w_up"], prm["w_down"], prm["ln2_g"], prm["ln2_b"],
             prm["alpha"], tm, _largest_divisor(prm["w_gate"].shape[1], 512, LANE))
    return y.reshape(bsz, seqlen, d), conv_new, s_new, x_new


def kernel(x_prompt, x_sample, state_conv_qkv, state_delta, state_s5_re, state_s5_im, meta_tokens, ln_in_g, ln_in_b, w_in, conv_w, dn_a_log, dn_dt_bias, dn_norm_w, s5_a_re, s5_a_im, s5_log_dt, s5_b_re, s5_b_im, s5_c_re, s5_c_im, s5_d, s5_w_glu, s5_b_glu, w_out, ln1_g, ln1_b, ffn_w_gate, ffn_w_up, ffn_w_down, ln2_g, ln2_b):
    depth = w_in.shape[0]
    assert depth == 1, "single-layer trunk only"
    bp, seq, d = x_prompt.shape
    bs, dec_seq, _ = x_sample.shape
    n_meta = meta_tokens.shape[0]
    qkv_dim = conv_w.shape[-1]
    dn = qkv_dim // 3
    heads = dn // HEAD_DIM
    groups, sdim = s5_a_re.shape[1:]
    width = s5_d.shape[-1]
    assert n_meta == dec_seq and seq % CHUNK == 0 and width == groups * S5_GROUP_CH
    alpha = (2.0 * depth) ** 0.25
    row = lambda v: v.reshape(1, -1).astype(F32)

    w = w_in[0]
    n_gate = 2 * heads
    w_qkvz = w[:, :qkv_dim + dn].astype(BF16)
    w_u = w[:, qkv_dim + dn + n_gate:].astype(BF16)
    w_ab = jnp.pad(w[:, qkv_dim + dn:qkv_dim + dn + n_gate], ((0, 0), (0, LANE - n_gate))).astype(BF16)
    gate8 = jnp.zeros((SUBLANE, LANE), F32)
    gate8 = gate8.at[0, :heads].set(dn_a_log[0]).at[1, :heads].set(dn_dt_bias[0])
    cw8 = jnp.pad(conv_w[0].astype(F32), ((0, SUBLANE - CONV_W), (0, 0)))

    lam_re, lam_im, f_re, f_im = _s5_disc(s5_a_re[0], s5_a_im[0], s5_log_dt[0])
    bb_re = f_re[..., None] * s5_b_re[0] - f_im[..., None] * s5_b_im[0]
    bb_im = f_re[..., None] * s5_b_im[0] + f_im[..., None] * s5_b_re[0]
    gk = GROUPS_PER_KTILE
    kt = groups // gk
    eye = jnp.eye(gk, dtype=F32)
    bb5 = jnp.stack([bb_re, bb_im]).reshape(2, kt, gk, sdim, S5_GROUP_CH)
    bb = jnp.einsum("rkgph,gG->kGhrgp", bb5, eye).reshape(kt, gk * S5_GROUP_CH, 2 * gk * sdim).astype(BF16)
    cc5 = jnp.stack([s5_c_re[0], -s5_c_im[0]]).reshape(2, kt, gk, S5_GROUP_CH, sdim)
    cc = jnp.einsum("rkghp,gG->krgpGh", cc5, eye).reshape(kt, 2 * gk * sdim, gk * S5_GROUP_CH).astype(BF16)
    lam = jnp.concatenate([lam_re.reshape(kt, gk * sdim), lam_im.reshape(kt, gk * sdim)], axis=0)

    prm = dict(
        ln_in_g=row(ln_in_g), ln_in_b=row(ln_in_b), w_qkvz=w_qkvz, w_u=w_u, w_ab=w_ab, cw8=cw8, gate8=gate8,
        nw=row(dn_norm_w[0]), bb=bb, cc=cc, lam=lam, s5_d=row(s5_d[0]), w_glu=s5_w_glu[0].astype(BF16),
        b_glu=row(s5_b_glu[0]), w_out=w_out[0].astype(BF16), ln1_g=row(ln1_g[0]), ln1_b=row(ln1_b[0]),
        w_gate=ffn_w_gate[0].astype(BF16), w_up=ffn_w_up[0].astype(BF16),
        w_down=ffn_w_down[0].astype(BF16), ln2_g=row(ln2_g[0]), ln2_b=row(ln2_b[0]), alpha=alpha)

    def s5_pack(re, im):
        b = re.shape[0]
        return jnp.stack([re.reshape(b, kt, gk, sdim), im.reshape(b, kt, gk, sdim)], axis=2).reshape(b, -1)

    def s5_unpack(x):
        b = x.shape[0]
        x = x.reshape(b, kt, 2, gk, sdim)
        return x[:, :, 0].reshape(b, groups, sdim), x[:, :, 1].reshape(b, groups, sdim)

    nb = 2 * SUBLANE
    assert bs + 1 <= nb
    pad_b = lambda a: jnp.pad(a.astype(F32), ((0, nb - a.shape[0]),) + ((0, 0),) * (a.ndim - 1))
    x_small = pad_b(jnp.concatenate([x_sample, meta_tokens[None].astype(x_sample.dtype)], axis=0))
    conv_small = pad_b(jnp.pad(state_conv_qkv[0], ((0, 0), (SUBLANE - CONV_W + 1, 0), (0, 0))))
    s0_small = pad_b(state_delta[0].reshape(bs, dn, HEAD_DIM))
    x0_small = pad_b(s5_pack(state_s5_re[0], state_s5_im[0]))
    y_s, conv_s, s_s, xf_s = _layer(x_small, conv_small, s0_small, x0_small, prm,
                                    dec_seq, 1, heads, dec_seq)

    bc = lambda a: jnp.broadcast_to(a[bs:bs + 1], (bp,) + a.shape[1:])
    y_p, conv_p, s_p, xf_p = _layer(x_prompt, bc(conv_s), bc(s_s), bc(xf_s), prm,
                                    CHUNK, DELTA_CHUNKS_PER_STEP, 256 // CHUNK, CHUNK)

    re_p, im_p = s5_unpack(xf_p)
    re_s, im_s = s5_unpack(xf_s[:bs])
    tail = SUBLANE - CONV_W + 1
    return (y_p.astype(x_prompt.dtype), y_s[:bs].astype(x_sample.dtype),
            conv_p[None, :, tail:].astype(state_conv_qkv.dtype),
            s_p.reshape(bp, heads, HEAD_DIM, HEAD_DIM)[None].astype(state_delta.dtype),
            re_p[None].astype(state_s5_re.dtype), im_p[None].astype(state_s5_im.dtype),
            conv_s[None, :bs, tail:].astype(state_conv_qkv.dtype),
            s_s[:bs].reshape(bs, heads, HEAD_DIM, HEAD_DIM)[None].astype(state_delta.dtype),
            re_s[None].astype(state_s5_re.dtype), im_s[None].astype(state_s5_im.dtype))
```

```python
import functools
import math

import jax
import jax.numpy as jnp
import numpy as np
from jax import lax
from jax.experimental import pallas as pl
from jax.experimental.pallas import tpu as pltpu

LN_EPS = 1e-5
RMS_EPS = 1e-6
CHUNK = 64
HEAD_DIM = 128
DELTA_CHUNKS_PER_STEP = 4
CONV_W = 4
S5_GROUP_CH = 16
GROUPS_PER_KTILE = 16
LANE = 128
SUBLANE = 8
VMEM_LIMIT = 56 * 1024 * 1024
FFN_VMEM_LIMIT = 60 * 1024 * 1024

F32 = jnp.float32
BF16 = jnp.bfloat16


def _layer_norm(x, g, b):
    mu = jnp.mean(x, axis=-1, keepdims=True)
    xc = x - mu
    var = jnp.mean(xc * xc, axis=-1, keepdims=True)
    return xc * lax.rsqrt(var + LN_EPS) * g + b


def _sigmoid(x):
    return 0.5 + 0.5 * jnp.tanh(0.5 * x)


def _silu(x):
    h = 0.5 * x
    return h + h * jnp.tanh(h)


def _softplus(x):
    return jnp.maximum(x, 0.0) + jnp.log1p(jnp.exp(-jnp.abs(x)))


def _ln_inproj_kernel(x_ref, g_ref, b_ref, w_ref, wu_ref, wab_ref, proj_ref, ab_ref, h_scr):
    j = pl.program_id(1)
    last = pl.num_programs(1) - 1

    @pl.when(j == 0)
    def _():
        h = _layer_norm(x_ref[...], g_ref[...], b_ref[...]).astype(BF16)
        h_scr[...] = h
        ab_ref[...] = jnp.dot(h, wab_ref[...], preferred_element_type=F32)

    @pl.when(j < last)
    def _():
        proj_ref[...] = jnp.dot(h_scr[...], w_ref[...], preferred_element_type=F32)

    @pl.when(j == last)
    def _():
        proj_ref[...] = jnp.dot(h_scr[...], wu_ref[...], preferred_element_type=F32)


def _ln_inproj(x2, ln_g, ln_b, w_qkvz, w_u, w_ab, tm):
    rows, d = x2.shape
    tn = w_u.shape[1]
    na = w_qkvz.shape[1] // tn
    assert w_qkvz.shape[1] == na * tn
    return pl.pallas_call(
        _ln_inproj_kernel,
        grid=(rows // tm, na + 1),
        in_specs=[
            pl.BlockSpec((tm, d), lambda i, j: (i, 0)),
            pl.BlockSpec((1, d), lambda i, j: (0, 0)),
            pl.BlockSpec((1, d), lambda i, j: (0, 0)),
            pl.BlockSpec((d, tn), lambda i, j: (0, jnp.minimum(j, na - 1))),
            pl.BlockSpec((d, tn), lambda i, j: (0, 0)),
            pl.BlockSpec((d, LANE), lambda i, j: (0, 0)),
        ],
        out_specs=[
            pl.BlockSpec((tm, tn), lambda i, j: (i, j)),
            pl.BlockSpec((tm, LANE), lambda i, j: (i, 0)),
        ],
        out_shape=[jax.ShapeDtypeStruct((rows, (na + 1) * tn), F32),
                   jax.ShapeDtypeStruct((rows, LANE), F32)],
        scratch_shapes=[pltpu.VMEM((tm, d), BF16)],
        compiler_params=pltpu.CompilerParams(
            dimension_semantics=("arbitrary", "arbitrary"), vmem_limit_bytes=VMEM_LIMIT),
        name="ln_inproj",
    )(x2, ln_g, ln_b, w_qkvz, w_u, w_ab)


def _delta_masks(C, HP):
    R = HP * C
    ri = np.arange(R)[:, None]
    ci = np.arange(R)[None, :]
    same = (ri // C) == (ci // C)
    rr = [same & (ri >= ci)]
    s = 2
    while s < C:
        rr.append(((ri // (2 * s)) == (ci // (2 * s))) & ((ri // s) % 2 == 1) & ((ci // s) % 2 == 0))
        s *= 2
    rr.append(((ri // 2) == (ci // 2)) & (ri % 2 == 1) & (ci % 2 == 0))
    rr.append(ri == ci)
    return jnp.asarray(np.stack(rr), BF16)


def _delta_kernel(qkv_ref, z_ref, ab_ref, cprev_ref, s0_ref, cw_ref, gate_ref, nw_ref,
                  mrr_ref,
                  o_ref, cnew_ref, sfin_ref, cbuf, s_scr, *, C, NCH, HP, H):
    t = pl.program_id(1)
    dn = H * HEAD_DIM
    R = HP * C
    SR = HP * HEAD_DIM
    TB = NCH * C
    tail = SUBLANE

    @pl.when(t == 0)
    def _():
        cbuf[0:tail, :] = cprev_ref[0]
        s_scr[...] = s0_ref[0]

    cbuf[tail:tail + TB, :] = qkv_ref[0]
    cw_half = 0.5 * cw_ref[...]

    def conv_cols(r0, c0):
        ext = cbuf[r0:r0 + tail + C, c0:c0 + HEAD_DIM]
        half = cw_half[CONV_W - 1:CONV_W, c0:c0 + HEAD_DIM] * ext[tail:, :]
        for i in range(CONV_W - 1):
            shifted = pltpu.roll(ext, CONV_W - 1 - i, 0)[tail:, :]
            half = half + shifted * cw_half[i:i + 1, c0:c0 + HEAD_DIM]
        return half + half * jnp.tanh(half)

    def l2n(x, scale=1.0):
        return x * (lax.rsqrt(jnp.sum(x * x, axis=-1, keepdims=True) + RMS_EPS) * scale)

    def block_diag(x):
        zero = jnp.zeros((C, HEAD_DIM), BF16)
        return jnp.concatenate(
            [jnp.concatenate([x[r * C:(r + 1) * C, :] if r % HP == b else zero for b in range(HP)], axis=1)
             for r in range(x.shape[0] // C)], axis=0)

    bf = lambda x: x.astype(BF16)
    mm = lambda a, b: jnp.dot(a, b, preferred_element_type=F32)
    rowc = lax.broadcasted_iota(jnp.int32, (C, LANE), 0)
    reps = LANE // C
    lane_rep = lax.broadcasted_iota(jnp.int32, (1, LANE), 1) // C

    chains = [(c, p) for c in range(NCH) for p in range(H // HP)]
    gates = {}
    for c in range(NCH):
        r0 = c * C
        ab = ab_ref[0, r0:r0 + C, :]
        g_all = -jnp.exp(gate_ref[0:1, :]) * _softplus(ab + gate_ref[1:2, :])
        beta_all = _sigmoid(ab)
        gcum_all = g_all
        s = 1
        while s < C:
            gcum_all = gcum_all + jnp.where(rowc >= s, pltpu.roll(gcum_all, s, 0), 0.0)
            s *= 2
        gates[c] = (beta_all, gcum_all, gcum_all[C - 1:C, :],
                    jnp.concatenate([gcum_all] * reps, axis=0).T)

    st = {}

    def prologue(c, p):
        r0 = c * C
        beta_all, gcum_all, glast_all, g_t = gates[c]
        heads = range(p * HP, (p + 1) * HP)
        qs = jnp.concatenate(
            [l2n(conv_cols(r0, h * HEAD_DIM), HEAD_DIM ** -0.5) for h in heads], axis=0)
        ks = jnp.concatenate([l2n(conv_cols(r0, dn + h * HEAD_DIM)) for h in heads], axis=0)
        vs = jnp.concatenate([conv_cols(r0, 2 * dn + h * HEAD_DIM) for h in heads], axis=0)
        beta_c = jnp.concatenate([beta_all[:, H + h:H + h + 1] for h in heads], axis=0)
        gc_c = jnp.concatenate([gcum_all[:, h:h + 1] for h in heads], axis=0)
        gl_c = jnp.concatenate(
            [jnp.broadcast_to(glast_all[:, h:h + 1], (C, 1)) for h in heads], axis=0)
        row_blocks = []
        for m in range(R // LANE):
            blk = g_t[p * HP + m * reps:p * HP + m * reps + 1, :]
            for r in range(1, reps):
                hr = p * HP + m * reps + r
                blk = jnp.where(lane_rep == r, g_t[hr:hr + 1, :], blk)
            row_blocks.append(blk)
        gc_r = jnp.concatenate(row_blocks, axis=1)
        kb = ks * beta_c
        gexp = jnp.exp(gc_c)
        g_tot = jnp.concatenate(
            [jnp.broadcast_to(jnp.exp(glast_all[:, h:h + 1]), (HEAD_DIM, 1)) for h in heads], axis=0)
        st[c, p] = dict(
            gc_c=gc_c, gc_r=gc_r, lhs=bf(jnp.concatenate([kb, qs], axis=0)), ks_b=bf(ks),
            rhs=bf(jnp.concatenate([vs * beta_c, kb * gexp], axis=1)), q_dec=qs * gexp,
            kdec_bd=block_diag(bf(ks * jnp.exp(gl_c - gc_c))), g_tot=g_tot)

    def gram_issue(chs):
        for ch in chs:
            d = st[ch]
            d["kkqk"] = lax.dot_general(d["lhs"], d["ks_b"], (((1,), (1,)), ((), ())),
                                        preferred_element_type=F32)

    n_lvl = mrr_ref.shape[0] - 2

    def lane_tiles(tiles):
        zero = jnp.zeros((LANE, LANE), BF16)
        return jnp.concatenate(
            [jnp.concatenate([t if a == b else zero for b in range(len(tiles))], axis=1)
             for a, t in enumerate(tiles)], axis=0)

    def gram_post(chs):
        for ch in chs:
            d = st[ch]
            a_t, qk_t = [], []
            for m in range(R // LANE):
                rows = slice(m * LANE, (m + 1) * LANE)
                e = jnp.exp(jnp.minimum(d["gc_c"][rows, :] - d["gc_r"][:, rows], 0.0))
                a_t.append(bf(d["kkqk"][rows, rows] * e))
                qk_t.append(bf(d["kkqk"][R + m * LANE:R + (m + 1) * LANE, rows] * e)
                            * mrr_ref[0, rows, rows])
            d["a_t"] = a_t
            d["qk_b"] = lane_tiles(qk_t)
            d["x_b"] = mrr_ref[n_lvl + 1] - lane_tiles(
                [t * mrr_ref[n_lvl, m * LANE:(m + 1) * LANE, m * LANE:(m + 1) * LANE]
                 for m, t in enumerate(a_t)])

    BF16_ROWS = 2 * SUBLANE

    def odd_rows(x, s):
        return jnp.concatenate([x[r:r + s, :] for r in range(s, x.shape[0], 2 * s)], axis=0)

    def level_first(chs, lvl):
        s = 2 ** lvl
        for ch in chs:
            d = st[ch]
            lhs = odd_rows(d["x_b"], s) if s >= BF16_ROWS else d["x_b"]
            d["p"] = mm(lhs, lane_tiles(
                [t * mrr_ref[lvl, m * LANE:(m + 1) * LANE, m * LANE:(m + 1) * LANE]
                 for m, t in enumerate(d["a_t"])]))

    def level_second(chs):
        for ch in chs:
            d = st[ch]
            d["q"] = mm(bf(d["p"]), d["x_b"])

    def level_post(chs, lvl):
        s = 2 ** lvl
        for ch in chs:
            d = st[ch]
            if s >= BF16_ROWS:
                x, q = d["x_b"], bf(d["q"])
                d["x_b"] = jnp.concatenate(
                    [x[r:r + s, :] if (r // s) % 2 == 0 else x[r:r + s, :] - q[(r // (2 * s)) * s:(r // (2 * s) + 1) * s, :]
                     for r in range(0, R, s)], axis=0)
            else:
                d["x_b"] = d["x_b"] - bf(d["q"])

    def solve_issue(chs):
        for ch in chs:
            d = st[ch]
            d["sol"] = mm(d["x_b"], d["rhs"])

    def solve_post(chs):
        for ch in chs:
            d = st[ch]
            d["kq_bd"] = block_diag(bf(jnp.concatenate([d["sol"][:, HEAD_DIM:], d["q_dec"]], axis=0)))

    packs = range(H // HP)
    rec = {}

    def rec_read(c):
        for p in packs:
            s_old = s_scr[p * SR:(p + 1) * SR, :]
            rec[c, p] = dict(s_old=s_old, ks_qs=mm(st[c, p]["kq_bd"], bf(s_old)))

    def rec_inner(c):
        for p in packs:
            d = rec[c, p]
            d["w_b"] = bf(st[c, p]["sol"][:, :HEAD_DIM] - d["ks_qs"][:R])
            d["o"] = mm(st[c, p]["qk_b"], d["w_b"])

    def rec_write(c):
        for p in packs:
            d = rec[c, p]
            s_scr[p * SR:(p + 1) * SR, :] = d["s_old"] * st[c, p]["g_tot"] + lax.dot_general(
                st[c, p]["kdec_bd"], d["w_b"], (((0,), (0,)), ((), ())), preferred_element_type=F32)

    def rec_out(c):
        r0 = c * C
        for p in packs:
            d = rec[c, p]
            o_p = d["ks_qs"][R:] + d["o"]
            for i in range(HP):
                h = p * HP + i
                oh = o_p[i * C:(i + 1) * C, :]
                oh = oh * lax.rsqrt(jnp.mean(oh * oh, axis=-1, keepdims=True) + RMS_EPS) * nw_ref[...]
                zz = z_ref[0, r0:r0 + C, h * HEAD_DIM:(h + 1) * HEAD_DIM]
                o_ref[0, r0:r0 + C, h * HEAD_DIM:(h + 1) * HEAD_DIM] = (oh * _silu(zz)).astype(o_ref.dtype)

    group = max(NCH // 2, 1) * len(packs)
    groups = [chains[g:g + group] for g in range(0, len(chains), group)]
    fillers = []
    for ch in groups[0]:
        prologue(*ch)
    for g, chs in enumerate(groups):
        if g + 1 < len(groups):
            fillers = [functools.partial(prologue, *ch) for ch in groups[g + 1]] + fillers
        fill = lambda: fillers.pop(0)() if fillers else None
        gram_issue(chs)
        gram_post(chs)
        fill()
        for lvl in range(1, n_lvl):
            level_first(chs, lvl)
            level_second(chs)
            level_post(chs, lvl)
            fill()
        solve_issue(chs)
        solve_post(chs)
        while fillers:
            fill()
        for c in sorted({c for c, _ in chs}):
            fillers += [functools.partial(rec_read, c), functools.partial(rec_inner, c),
                        lambda c=c: (rec_write(c), rec_out(c))]
    while fillers:
        fill()

    cbuf[0:tail, :] = cbuf[TB:TB + tail, :]
    cnew_ref[0] = cbuf[0:tail, :]

    @pl.when(t == pl.num_programs(1) - 1)
    def _():
        sfin_ref[0] = s_scr[...]


def _delta(proj3, ab3, conv_prev, s0, cw8, gate8, nw, C, NCH, HP):
    bsz, seqlen, _ = proj3.shape
    qkv_dim = cw8.shape[1]
    dn = qkv_dim // 3
    heads = dn // HEAD_DIM
    tb = NCH * C
    kern = functools.partial(_delta_kernel, C=C, NCH=NCH, HP=HP, H=heads)
    masks = _delta_masks(C, HP)
    const = lambda a: pl.BlockSpec(a.shape, lambda b, t: (0,) * a.ndim)
    return pl.pallas_call(
        kern,
        grid=(bsz, seqlen // tb),
        in_specs=[
            pl.BlockSpec((1, tb, qkv_dim), lambda b, t: (b, t, 0)),
            pl.BlockSpec((1, tb, dn), lambda b, t: (b, t, qkv_dim // dn)),
            pl.BlockSpec((1, tb, LANE), lambda b, t: (b, t, 0)),
            pl.BlockSpec((1, SUBLANE, qkv_dim), lambda b, t: (b, 0, 0)),
            pl.BlockSpec((1, dn, HEAD_DIM), lambda b, t: (b, 0, 0)),
            pl.BlockSpec((SUBLANE, qkv_dim), lambda b, t: (0, 0)),
            pl.BlockSpec((SUBLANE, LANE), lambda b, t: (0, 0)),
            pl.BlockSpec((1, HEAD_DIM), lambda b, t: (0, 0)),
            const(masks),
        ],
        out_specs=[
            pl.BlockSpec((1, tb, dn), lambda b, t: (b, t, 0)),
            pl.BlockSpec((1, SUBLANE, qkv_dim), lambda b, t: (b, 0, 0)),
            pl.BlockSpec((1, dn, HEAD_DIM), lambda b, t: (b, 0, 0)),
        ],
        out_shape=[
            jax.ShapeDtypeStruct((bsz, seqlen, dn), BF16),
            jax.ShapeDtypeStruct((bsz, SUBLANE, qkv_dim), F32),
            jax.ShapeDtypeStruct((bsz, dn, HEAD_DIM), F32),
        ],
        scratch_shapes=[pltpu.VMEM((SUBLANE + tb, qkv_dim), F32), pltpu.VMEM((dn, HEAD_DIM), F32)],
        compiler_params=pltpu.CompilerParams(
            dimension_semantics=("arbitrary", "arbitrary"), vmem_limit_bytes=VMEM_LIMIT),
        name="delta_rule",
    )(proj3, proj3, ab3, conv_prev, s0, cw8, gate8, nw, masks)


def _s5_disc_kernel(are_ref, aim_ref, logdt_ref, lre_ref, lim_ref, fre_ref, fim_ref):
    a_re = are_ref[...]
    a_im = aim_ref[...]
    dt = jnp.exp(logdt_ref[...])
    mag = jnp.exp(a_re * dt)
    lam_re = mag * jnp.cos(a_im * dt)
    lam_im = mag * jnp.sin(a_im * dt)
    den = a_re * a_re + a_im * a_im
    nr = lam_re - 1.0
    lre_ref[...] = lam_re
    lim_ref[...] = lam_im
    fre_ref[...] = (nr * a_re + lam_im * a_im) / den
    fim_ref[...] = (lam_im * a_re - nr * a_im) / den


def _s5_disc(a_re, a_im, log_dt):
    shp = jax.ShapeDtypeStruct(a_re.shape, F32)
    return pl.pallas_call(_s5_disc_kernel, out_shape=[shp] * 4, name="s5_disc")(
        a_re, a_im, log_dt.reshape(-1, 1))


def _s5_kernel(u_ref, x0_ref, bb_ref, cc_ref, lam_ref, d_ref, wg_ref, bg_ref,
               y_ref, xfin_ref, utm, xs, ytm, st, *, B, TB, KT, SW):
    t = pl.program_id(0)
    kw = 2 * SW
    cw = u_ref.shape[2] // KT

    @pl.when(t == 0)
    def _():
        st[...] = x0_ref[...]

    nl = u_ref.shape[2] // LANE
    for b in range(B):
        for j in range(nl):
            utm[j, pl.ds(b, TB, stride=B), :] = u_ref[b, :, j * LANE:(j + 1) * LANE]
    u_t = jnp.concatenate([utm[j] for j in range(nl)], axis=1)
    ub = u_t.astype(BF16)

    def input_map(kt):
        xs[:, kt * kw:(kt + 1) * kw] = jnp.dot(ub[:, kt * cw:(kt + 1) * cw], bb_ref[kt],
                                               preferred_element_type=F32)

    def recurrence(kt):
        half = SW // 2
        for hf in range(2):
            c_re = kt * kw + hf * half
            c_im = c_re + SW
            l_re = jnp.broadcast_to(lam_ref[kt:kt + 1, hf * half:(hf + 1) * half], (B, half))
            l_im = jnp.broadcast_to(lam_ref[KT + kt:KT + kt + 1, hf * half:(hf + 1) * half], (B, half))
            x_re = st[:, c_re:c_re + half]
            x_im = st[:, c_im:c_im + half]
            for tt in range(TB):
                rows = slice(tt * B, (tt + 1) * B)
                n_re = l_re * x_re - l_im * x_im + xs[rows, c_re:c_re + half]
                n_im = l_re * x_im + l_im * x_re + xs[rows, c_im:c_im + half]
                xs[rows, c_re:c_re + half] = n_re
                xs[rows, c_im:c_im + half] = n_im
                x_re, x_im = n_re, n_im
            st[:, c_re:c_re + half] = x_re
            st[:, c_im:c_im + half] = x_im

    def output_map(kt):
        return jnp.dot(xs[:, kt * kw:(kt + 1) * kw].astype(BF16), cc_ref[kt],
                       preferred_element_type=F32)

    input_map(0)
    ys = []
    for kt in range(KT):
        if kt + 1 < KT:
            input_map(kt + 1)
        recurrence(kt)
        cols = slice(kt * cw, (kt + 1) * cw)
        y = output_map(kt) + d_ref[:, cols] * u_t[:, cols]
        ys.append(0.5 * y * (1.0 + jnp.tanh(math.sqrt(2.0 / math.pi) * (y + 0.044715 * (y * y * y)))))
    y = jnp.concatenate(ys, axis=1)
    y = y * _sigmoid(jnp.dot(y.astype(BF16), wg_ref[...], preferred_element_type=F32) + bg_ref[...])
    for j in range(nl):
        ytm[j] = y[:, j * LANE:(j + 1) * LANE]
    for b in range(B):
        for j in range(nl):
            y_ref[b, :, j * LANE:(j + 1) * LANE] = ytm[j, pl.ds(b, TB, stride=B), :].astype(y_ref.dtype)

    @pl.when(t == pl.num_programs(0) - 1)
    def _():
        xfin_ref[...] = st[...]


def _s5(proj3, x0, bb, cc, lam, d_row, w_glu, b_glu, TB):
    bsz, seqlen, n = proj3.shape
    width = d_row.shape[1]
    kt, cw, kw = bb.shape
    sdim = x0.shape[1]
    const = lambda shape: pl.BlockSpec(shape, lambda t: (0,) * len(shape))
    kern = functools.partial(_s5_kernel, B=bsz, TB=TB, KT=kt, SW=kw // 2)
    return pl.pallas_call(
        kern,
        grid=(seqlen // TB,),
        in_specs=[
            pl.BlockSpec((bsz, TB, width), lambda t: (0, t, n // width - 1)),
            const((bsz, sdim)), const((kt, cw, kw)), const((kt, kw, cw)), const(lam.shape),
            const((1, width)), const((width, width)), const((1, width)),
        ],
        out_specs=[pl.BlockSpec((bsz, TB, width), lambda t: (0, t, 0)), const((bsz, sdim))],
        out_shape=[jax.ShapeDtypeStruct((bsz, seqlen, width), BF16),
                   jax.ShapeDtypeStruct((bsz, sdim), F32)],
        scratch_shapes=[pltpu.VMEM((width // LANE, bsz * TB, LANE), F32),
                        pltpu.VMEM((bsz * TB, sdim), F32),
                        pltpu.VMEM((width // LANE, bsz * TB, LANE), F32),
                        pltpu.VMEM((bsz, sdim), F32)],
        compiler_params=pltpu.CompilerParams(
            dimension_semantics=("arbitrary",), vmem_limit_bytes=VMEM_LIMIT),
        name="s5",
    )(proj3, x0, bb, cc, lam, d_row, w_glu, b_glu)


def _out_ln_kernel(x_ref, o_ref, y5_ref, lng_ref, lnb_ref, wout_ref, g1_ref, b1_ref, x1_ref,
                   *, alpha):
    dn = o_ref.shape[1]
    h = _layer_norm(x_ref[...], lng_ref[...], lnb_ref[...])
    mix = (jnp.dot(o_ref[...], wout_ref[0:dn, :], preferred_element_type=F32)
           + jnp.dot(y5_ref[...], wout_ref[dn:, :], preferred_element_type=F32))
    x1_ref[...] = _layer_norm(alpha * h + mix, g1_ref[...], b1_ref[...])


def _out_ln(x2, o2, y52, ln_g, ln_b, w_out, g1, b1, alpha, tm):
    rows, d = x2.shape
    row = lambda w: pl.BlockSpec((tm, w), lambda i: (i, 0))
    vec = pl.BlockSpec((1, d), lambda i: (0, 0))
    return pl.pallas_call(
        functools.partial(_out_ln_kernel, alpha=alpha),
        grid=(rows // tm,),
        in_specs=[row(d), row(o2.shape[1]), row(y52.shape[1]), vec, vec,
                  pl.BlockSpec(w_out.shape, lambda i: (0, 0)), vec, vec],
        out_specs=row(d),
        out_shape=jax.ShapeDtypeStruct((rows, d), F32),
        compiler_params=pltpu.CompilerParams(
            dimension_semantics=("arbitrary",), vmem_limit_bytes=VMEM_LIMIT),
        name="out_ln",
    )(x2, o2, y52, ln_g, ln_b, w_out, g1, b1)


def _ffn_kernel(x1_ref, wg_ref, wu_ref, wd_ref, g2_ref, b2_ref, y_ref, x1b, *, alpha):
    j = pl.program_id(1)

    @pl.when(j == 0)
    def _():
        x1 = x1_ref[...]
        x1b[...] = x1.astype(BF16)
        y_ref[...] = alpha * x1

    xb = x1b[...]
    gate = jnp.dot(xb, wg_ref[...], preferred_element_type=F32)
    up = jnp.dot(xb, wu_ref[...], preferred_element_type=F32)
    hb = (_silu(gate) * up).astype(BF16)
    th = wd_ref.shape[0]
    for c in range(0, y_ref.shape[1], th):
        y_ref[:, c:c + th] += jnp.dot(hb, wd_ref[:, c:c + th], preferred_element_type=F32)

    @pl.when(j == pl.num_programs(1) - 1)
    def _():
        y_ref[...] = _layer_norm(y_ref[...], g2_ref[...], b2_ref[...])


def _ffn(x1, w_gate, w_up, w_down, g2, b2, alpha, tm, th):
    rows, d = x1.shape
    fh = w_gate.shape[1]
    vec = pl.BlockSpec((1, d), lambda i, j: (0, 0))
    return pl.pallas_call(
        functools.partial(_ffn_kernel, alpha=alpha),
        grid=(rows // tm, fh // th),
        in_specs=[
            pl.BlockSpec((tm, d), lambda i, j: (i, 0)),
            pl.BlockSpec((d, th), lambda i, j: (0, j)),
            pl.BlockSpec((d, th), lambda i, j: (0, j)),
            pl.BlockSpec((th, d), lambda i, j: (j, 0)),
            vec, vec,
        ],
        out_specs=pl.BlockSpec((tm, d), lambda i, j: (i, 0)),
        out_shape=jax.ShapeDtypeStruct((rows, d), F32),
        scratch_shapes=[pltpu.VMEM((tm, d), BF16)],
        compiler_params=pltpu.CompilerParams(
            dimension_semantics=("arbitrary", "arbitrary"), vmem_limit_bytes=FFN_VMEM_LIMIT),
        name="ffn",
    )(x1, w_gate, w_up, w_down, g2, b2)


def _largest_divisor(n, cap, mult):
    best = mult
    for c in range(mult, cap + 1, mult):
        if n % c == 0:
            best = c
    return best


def _layer(x3, conv_prev8, s0, x0, prm, chunk, chunks_per_step, heads_per_pack, s5_tb):
    bsz, seqlen, d = x3.shape
    rows = bsz * seqlen
    x2 = x3.reshape(rows, d)
    tm = _largest_divisor(rows, 1024, 16)
    proj, ab = _ln_inproj(x2, prm["ln_in_g"], prm["ln_in_b"], prm["w_qkvz"], prm["w_u"], prm["w_ab"], tm)
    proj3 = proj.reshape(bsz, seqlen, -1)
    o, conv_new, s_new = _delta(proj3, ab.reshape(bsz, seqlen, LANE), conv_prev8, s0,
                                prm["cw8"], prm["gate8"], prm["nw"], chunk, chunks_per_step,
                                heads_per_pack)
    y5, x_new = _s5(proj3, x0, prm["bb"], prm["cc"], prm["lam"], prm["s5_d"], prm["w_glu"],
                    prm["b_glu"], s5_tb)
    x1 = _out_ln(x2, o.reshape(rows, -1), y5.reshape(rows, -1), prm["ln_in_g"], prm["ln_in_b"],
                 prm["w_out"], prm["ln1_g"], prm["ln1_b"], prm["alpha"],
                 _largest_divisor(rows, 512, 16))
    y = _ffn(x1, prm["w_gate"], prm["w_up"], prm["w_down"], prm["ln2_g"], prm["ln2_b"],
             prm["alpha"], tm, _largest_divisor(prm["w_gate"].shape[1], 512, LANE))
    return y.reshape(bsz, seqlen, d), conv_new, s_new, x_new


def kernel(x_prompt, x_sample, state_conv_qkv, state_delta, state_s5_re, state_s5_im, meta_tokens, ln_in_g, ln_in_b, w_in, conv_w, dn_a_log, dn_dt_bias, dn_norm_w, s5_a_re, s5_a_im, s5_log_dt, s5_b_re, s5_b_im, s5_c_re, s5_c_im, s5_d, s5_w_glu, s5_b_glu, w_out, ln1_g, ln1_b, ffn_w_gate, ffn_w_up, ffn_w_down, ln2_g, ln2_b):
    depth = w_in.shape[0]
    assert depth == 1, "single-layer trunk only"
    bp, seq, d = x_prompt.shape
    bs, dec_seq, _ = x_sample.shape
    n_meta = meta_tokens.shape[0]
    qkv_dim = conv_w.shape[-1]
    dn = qkv_dim // 3
    heads = dn // HEAD_DIM
    groups, sdim = s5_a_re.shape[1:]
    width = s5_d.shape[-1]
    assert n_meta == dec_seq and seq % CHUNK == 0 and width == groups * S5_GROUP_CH
    alpha = (2.0 * depth) ** 0.25
    row = lambda v: v.reshape(1, -1).astype(F32)

    w = w_in[0]
    n_gate = 2 * heads
    w_qkvz = w[:, :qkv_dim + dn].astype(BF16)
    w_u = w[:, qkv_dim + dn + n_gate:].astype(BF16)
    w_ab = jnp.pad(w[:, qkv_dim + dn:qkv_dim + dn + n_gate], ((0, 0), (0, LANE - n_gate))).astype(BF16)
    gate8 = jnp.zeros((SUBLANE, LANE), F32)
    gate8 = gate8.at[0, :heads].set(dn_a_log[0]).at[1, :heads].set(dn_dt_bias[0])
    cw8 = jnp.pad(conv_w[0].astype(F32), ((0, SUBLANE - CONV_W), (0, 0)))

    lam_re, lam_im, f_re, f_im = _s5_disc(s5_a_re[0], s5_a_im[0], s5_log_dt[0])
    bb_re = f_re[..., None] * s5_b_re[0] - f_im[..., None] * s5_b_im[0]
    bb_im = f_re[..., None] * s5_b_im[0] + f_im[..., None] * s5_b_re[0]
    gk = GROUPS_PER_KTILE
    kt = groups // gk
    eye = jnp.eye(gk, dtype=F32)
    bb5 = jnp.stack([bb_re, bb_im]).reshape(2, kt, gk, sdim, S5_GROUP_CH)
    bb = jnp.einsum("rkgph,gG->kGhrgp", bb5, eye).reshape(kt, gk * S5_GROUP_CH, 2 * gk * sdim).astype(BF16)
    cc5 = jnp.stack([s5_c_re[0], -s5_c_im[0]]).reshape(2, kt, gk, S5_GROUP_CH, sdim)
    cc = jnp.einsum("rkghp,gG->krgpGh", cc5, eye).reshape(kt, 2 * gk * sdim, gk * S5_GROUP_CH).astype(BF16)
    lam = jnp.concatenate([lam_re.reshape(kt, gk * sdim), lam_im.reshape(kt, gk * sdim)], axis=0)

    prm = dict(
        ln_in_g=row(ln_in_g), ln_in_b=row(ln_in_b), w_qkvz=w_qkvz, w_u=w_u, w_ab=w_ab, cw8=cw8, gate8=gate8,
        nw=row(dn_norm_w[0]), bb=bb, cc=cc, lam=lam, s5_d=row(s5_d[0]), w_glu=s5_w_glu[0].astype(BF16),
        b_glu=row(s5_b_glu[0]), w_out=w_out[0].astype(BF16), ln1_g=row(ln1_g[0]), ln1_b=row(ln1_b[0]),
        w_gate=ffn_w_gate[0].astype(BF16), w_up=ffn_w_up[0].astype(BF16),
        w_down=ffn_w_down[0].astype(BF16), ln2_g=row(ln2_g[0]), ln2_b=row(ln2_b[0]), alpha=alpha)

    def s5_pack(re, im):
        b = re.shape[0]
        return jnp.stack([re.reshape(b, kt, gk, sdim), im.reshape(b, kt, gk, sdim)], axis=2).reshape(b, -1)

    def s5_unpack(x):
        b = x.shape[0]
        x = x.reshape(b, kt, 2, gk, sdim)
        return x[:, :, 0].reshape(b, groups, sdim), x[:, :, 1].reshape(b, groups, sdim)

    nb = 2 * SUBLANE
    assert bs + 1 <= nb
    pad_b = lambda a: jnp.pad(a.astype(F32), ((0, nb - a.shape[0]),) + ((0, 0),) * (a.ndim - 1))
    x_small = pad_b(jnp.concatenate([x_sample, meta_tokens[None].astype(x_sample.dtype)], axis=0))
    conv_small = pad_b(jnp.pad(state_conv_qkv[0], ((0, 0), (SUBLANE - CONV_W + 1, 0), (0, 0))))
    s0_small = pad_b(state_delta[0].reshape(bs, dn, HEAD_DIM))
    x0_small = pad_b(s5_pack(state_s5_re[0], state_s5_im[0]))
    y_s, conv_s, s_s, xf_s = _layer(x_small, conv_small, s0_small, x0_small, prm,
                                    dec_seq, 1, heads, dec_seq)

    bc = lambda a: jnp.broadcast_to(a[bs:bs + 1], (bp,) + a.shape[1:])
    y_p, conv_p, s_p, xf_p = _layer(x_prompt, bc(conv_s), bc(s_s), bc(xf_s), prm,
                                    CHUNK, DELTA_CHUNKS_PER_STEP, 256 // CHUNK, CHUNK)

    re_p, im_p = s5_unpack(xf_p)
    re_s, im_s = s5_unpack(xf_s[:bs])
    tail = SUBLANE - CONV_W + 1
    return (y_p.astype(x_prompt.dtype), y_s[:bs].astype(x_sample.dtype),
            conv_p[None, :, tail:].astype(state_conv_qkv.dtype),
            s_p.reshape(bp, heads, HEAD_DIM, HEAD_DIM)[None].astype(state_delta.dtype),
            re_p[None].astype(state_s5_re.dtype), im_p[None].astype(state_s5_im.dtype),
            conv_s[None, :bs, tail:].astype(state_conv_qkv.dtype),
            s_s[:bs].reshape(bs, heads, HEAD_DIM, HEAD_DIM)[None].astype(state_delta.dtype),
            re_s[None].astype(state_s5_re.dtype), im_s[None].astype(state_s5_im.dtype))
```

```python
import functools
import math

import jax
import jax.numpy as jnp
import numpy as np
from jax import lax
from jax.experimental import pallas as pl
from jax.experimental.pallas import tpu as pltpu

LN_EPS = 1e-5
RMS_EPS = 1e-6
CHUNK = 64
HEAD_DIM = 128
DELTA_CHUNKS_PER_STEP = 4
DELTA_PACK_ROWS = 128
CONV_W = 4
S5_GROUP_CH = 16
GROUPS_PER_KTILE = 16
LANE = 128
SUBLANE = 8
VMEM_LIMIT = 56 * 1024 * 1024
FFN_VMEM_LIMIT = 60 * 1024 * 1024

F32 = jnp.float32
BF16 = jnp.bfloat16


def _layer_norm(x, g, b):
    mu = jnp.mean(x, axis=-1, keepdims=True)
    xc = x - mu
    var = jnp.mean(xc * xc, axis=-1, keepdims=True)
    return xc * lax.rsqrt(var + LN_EPS) * g + b


def _sigmoid(x):
    return 0.5 + 0.5 * jnp.tanh(0.5 * x)


def _silu(x):
    h = 0.5 * x
    return h + h * jnp.tanh(h)


def _softplus(x):
    return jnp.maximum(x, 0.0) + jnp.log1p(jnp.exp(-jnp.abs(x)))


def _ln_inproj_kernel(x_ref, g_ref, b_ref, w_ref, wu_ref, wab_ref, proj_ref, ab_ref, h_scr):
    j = pl.program_id(1)
    last = pl.num_programs(1) - 1

    @pl.when(j == 0)
    def _():
        h = _layer_norm(x_ref[...], g_ref[...], b_ref[...]).astype(BF16)
        h_scr[...] = h
        ab_ref[...] = jnp.dot(h, wab_ref[...], preferred_element_type=F32)

    @pl.when(j < last)
    def _():
        proj_ref[...] = jnp.dot(h_scr[...], w_ref[...], preferred_element_type=F32)

    @pl.when(j == last)
    def _():
        proj_ref[...] = jnp.dot(h_scr[...], wu_ref[...], preferred_element_type=F32)


def _ln_inproj(x2, ln_g, ln_b, w_qkvz, w_u, w_ab, tm):
    rows, d = x2.shape
    tn = w_u.shape[1]
    na = w_qkvz.shape[1] // tn
    assert w_qkvz.shape[1] == na * tn
    return pl.pallas_call(
        _ln_inproj_kernel,
        grid=(rows // tm, na + 1),
        in_specs=[
            pl.BlockSpec((tm, d), lambda i, j: (i, 0)),
            pl.BlockSpec((1, d), lambda i, j: (0, 0)),
            pl.BlockSpec((1, d), lambda i, j: (0, 0)),
            pl.BlockSpec((d, tn), lambda i, j: (0, jnp.minimum(j, na - 1))),
            pl.BlockSpec((d, tn), lambda i, j: (0, 0)),
            pl.BlockSpec((d, LANE), lambda i, j: (0, 0)),
        ],
        out_specs=[
            pl.BlockSpec((tm, tn), lambda i, j: (i, j)),
            pl.BlockSpec((tm, LANE), lambda i, j: (i, 0)),
        ],
        out_shape=[jax.ShapeDtypeStruct((rows, (na + 1) * tn), F32),
                   jax.ShapeDtypeStruct((rows, LANE), F32)],
        scratch_shapes=[pltpu.VMEM((tm, d), BF16)],
        compiler_params=pltpu.CompilerParams(
            dimension_semantics=("arbitrary", "arbitrary"), vmem_limit_bytes=VMEM_LIMIT),
        name="ln_inproj",
    )(x2, ln_g, ln_b, w_qkvz, w_u, w_ab)


def _delta_masks(C, HP):
    R = HP * C
    ri = np.arange(R)[:, None]
    ci = np.arange(R)[None, :]
    same = (ri // C) == (ci // C)
    rr = [same & (ri >= ci)]
    s = 2
    while s < C:
        rr.append(((ri // (2 * s)) == (ci // (2 * s))) & ((ri // s) % 2 == 1) & ((ci // s) % 2 == 0))
        s *= 2
    rr.append(((ri // 2) == (ci // 2)) & (ri % 2 == 1) & (ci % 2 == 0))
    rr.append(ri == ci)
    return jnp.asarray(np.stack(rr), BF16)


def _delta_kernel(qkv_ref, z_ref, ab_ref, cprev_ref, s0_ref, cw_ref, gate_ref, nw_ref,
                  mrr_ref,
                  o_ref, cnew_ref, sfin_ref, cbuf, s_scr, *, C, NCH, HP, H):
    t = pl.program_id(1)
    dn = H * HEAD_DIM
    R = HP * C
    SR = HP * HEAD_DIM
    TB = NCH * C
    tail = SUBLANE

    @pl.when(t == 0)
    def _():
        cbuf[0:tail, :] = cprev_ref[0]
        s_scr[...] = s0_ref[0]

    cbuf[tail:tail + TB, :] = qkv_ref[0]
    cw_half = 0.5 * cw_ref[...]

    def conv_cols(r0, c0):
        ext = cbuf[r0:r0 + tail + C, c0:c0 + HEAD_DIM]
        half = cw_half[CONV_W - 1:CONV_W, c0:c0 + HEAD_DIM] * ext[tail:, :]
        for i in range(CONV_W - 1):
            shifted = pltpu.roll(ext, CONV_W - 1 - i, 0)[tail:, :]
            half = half + shifted * cw_half[i:i + 1, c0:c0 + HEAD_DIM]
        return half + half * jnp.tanh(half)

    def l2n(x, scale=1.0):
        return x * (lax.rsqrt(jnp.sum(x * x, axis=-1, keepdims=True) + RMS_EPS) * scale)

    def block_diag(x):
        zero = jnp.zeros((C, HEAD_DIM), BF16)
        return jnp.concatenate(
            [jnp.concatenate([x[r * C:(r + 1) * C, :] if r % HP == b else zero for b in range(HP)], axis=1)
             for r in range(x.shape[0] // C)], axis=0)

    bf = lambda x: x.astype(BF16)
    mm = lambda a, b: jnp.dot(a, b, preferred_element_type=F32)
    rowc = lax.broadcasted_iota(jnp.int32, (C, LANE), 0)
    reps = LANE // C
    lane_rep = lax.broadcasted_iota(jnp.int32, (1, LANE), 1) // C

    chains = [(c, p) for c in range(NCH) for p in range(H // HP)]
    gates = {}
    for c in range(NCH):
        r0 = c * C
        ab = ab_ref[0, r0:r0 + C, :]
        g_all = -jnp.exp(gate_ref[0:1, :]) * _softplus(ab + gate_ref[1:2, :])
        beta_all = _sigmoid(ab)
        gcum_all = g_all
        s = 1
        while s < C:
            gcum_all = gcum_all + jnp.where(rowc >= s, pltpu.roll(gcum_all, s, 0), 0.0)
            s *= 2
        gates[c] = (beta_all, gcum_all, gcum_all[C - 1:C, :],
                    jnp.concatenate([gcum_all] * reps, axis=0).T)

    st = {}

    def prologue(c, p):
        r0 = c * C
        beta_all, gcum_all, glast_all, g_t = gates[c]
        heads = range(p * HP, (p + 1) * HP)
        qs = jnp.concatenate(
            [l2n(conv_cols(r0, h * HEAD_DIM), HEAD_DIM ** -0.5) for h in heads], axis=0)
        ks = jnp.concatenate([l2n(conv_cols(r0, dn + h * HEAD_DIM)) for h in heads], axis=0)
        vs = jnp.concatenate([conv_cols(r0, 2 * dn + h * HEAD_DIM) for h in heads], axis=0)
        beta_c = jnp.concatenate([beta_all[:, H + h:H + h + 1] for h in heads], axis=0)
        gc_c = jnp.concatenate([gcum_all[:, h:h + 1] for h in heads], axis=0)
        gl_c = jnp.concatenate(
            [jnp.broadcast_to(glast_all[:, h:h + 1], (C, 1)) for h in heads], axis=0)
        row_blocks = []
        for m in range(R // LANE):
            blk = g_t[p * HP + m * reps:p * HP + m * reps + 1, :]
            for r in range(1, reps):
                hr = p * HP + m * reps + r
                blk = jnp.where(lane_rep == r, g_t[hr:hr + 1, :], blk)
            row_blocks.append(blk)
        gc_r = jnp.concatenate(row_blocks, axis=1)
        kb = ks * beta_c
        gexp = jnp.exp(gc_c)
        g_tot = jnp.concatenate(
            [jnp.broadcast_to(jnp.exp(glast_all[:, h:h + 1]), (HEAD_DIM, 1)) for h in heads], axis=0)
        st[c, p] = dict(
            gc_c=gc_c, gc_r=gc_r, lhs=bf(jnp.concatenate([kb, qs], axis=0)), ks_b=bf(ks),
            rhs=bf(jnp.concatenate([vs * beta_c, kb * gexp], axis=1)), q_dec=qs * gexp,
            kdec_bd=block_diag(bf(ks * jnp.exp(gl_c - gc_c))), g_tot=g_tot)

    def gram_issue(chs):
        for ch in chs:
            d = st[ch]
            d["kkqk"] = lax.dot_general(d["lhs"], d["ks_b"], (((1,), (1,)), ((), ())),
                                        preferred_element_type=F32)

    n_lvl = mrr_ref.shape[0] - 2

    def lane_tiles(tiles):
        zero = jnp.zeros((LANE, LANE), BF16)
        return jnp.concatenate(
            [jnp.concatenate([t if a == b else zero for b in range(len(tiles))], axis=1)
             for a, t in enumerate(tiles)], axis=0)

    def gram_post(chs):
        for ch in chs:
            d = st[ch]
            a_t, qk_t = [], []
            for m in range(R // LANE):
                rows = slice(m * LANE, (m + 1) * LANE)
                e = jnp.exp(jnp.minimum(d["gc_c"][rows, :] - d["gc_r"][:, rows], 0.0))
                a_t.append(bf(d["kkqk"][rows, rows] * e))
                qk_t.append(bf(d["kkqk"][R + m * LANE:R + (m + 1) * LANE, rows] * e)
                            * mrr_ref[0, rows, rows])
            d["a_t"] = a_t
            d["qk_b"] = lane_tiles(qk_t)
            d["x_b"] = mrr_ref[n_lvl + 1] - lane_tiles(
                [t * mrr_ref[n_lvl, m * LANE:(m + 1) * LANE, m * LANE:(m + 1) * LANE]
                 for m, t in enumerate(a_t)])

    BF16_ROWS = 2 * SUBLANE

    def odd_rows(x, s):
        return jnp.concatenate([x[r:r + s, :] for r in range(s, x.shape[0], 2 * s)], axis=0)

    def level_first(chs, lvl):
        s = 2 ** lvl
        for ch in chs:
            d = st[ch]
            lhs = odd_rows(d["x_b"], s) if s >= BF16_ROWS else d["x_b"]
            d["p"] = mm(lhs, lane_tiles(
                [t * mrr_ref[lvl, m * LANE:(m + 1) * LANE, m * LANE:(m + 1) * LANE]
                 for m, t in enumerate(d["a_t"])]))

    def level_second(chs):
        for ch in chs:
            d = st[ch]
            d["q"] = mm(bf(d["p"]), d["x_b"])

    def level_post(chs, lvl):
        s = 2 ** lvl
        for ch in chs:
            d = st[ch]
            if s >= BF16_ROWS:
                x, q = d["x_b"], bf(d["q"])
                d["x_b"] = jnp.concatenate(
                    [x[r:r + s, :] if (r // s) % 2 == 0 else x[r:r + s, :] - q[(r // (2 * s)) * s:(r // (2 * s) + 1) * s, :]
                     for r in range(0, R, s)], axis=0)
            else:
                d["x_b"] = d["x_b"] - bf(d["q"])

    def solve_issue(chs):
        for ch in chs:
            d = st[ch]
            d["sol"] = mm(d["x_b"], d["rhs"])

    def solve_post(chs):
        for ch in chs:
            d = st[ch]
            d["kq_bd"] = block_diag(bf(jnp.concatenate([d["sol"][:, HEAD_DIM:], d["q_dec"]], axis=0)))

    packs = range(H // HP)
    rec = {}

    def rec_read(c):
        for p in packs:
            s_old = s_scr[p * SR:(p + 1) * SR, :]
            rec[c, p] = dict(s_old=s_old, ks_qs=mm(st[c, p]["kq_bd"], bf(s_old)))

    def rec_inner(c):
        for p in packs:
            d = rec[c, p]
            d["w_b"] = bf(st[c, p]["sol"][:, :HEAD_DIM] - d["ks_qs"][:R])
            d["o"] = mm(st[c, p]["qk_b"], d["w_b"])

    def rec_write(c):
        for p in packs:
            d = rec[c, p]
            s_scr[p * SR:(p + 1) * SR, :] = d["s_old"] * st[c, p]["g_tot"] + lax.dot_general(
                st[c, p]["kdec_bd"], d["w_b"], (((0,), (0,)), ((), ())), preferred_element_type=F32)

    def rec_out(c):
        r0 = c * C
        for p in packs:
            d = rec[c, p]
            o_p = d["ks_qs"][R:] + d["o"]
            for i in range(HP):
                h = p * HP + i
                oh = o_p[i * C:(i + 1) * C, :]
                oh = oh * lax.rsqrt(jnp.mean(oh * oh, axis=-1, keepdims=True) + RMS_EPS) * nw_ref[...]
                zz = z_ref[0, r0:r0 + C, h * HEAD_DIM:(h + 1) * HEAD_DIM]
                o_ref[0, r0:r0 + C, h * HEAD_DIM:(h + 1) * HEAD_DIM] = (oh * _silu(zz)).astype(o_ref.dtype)

    group = max(NCH // 2, 1) * len(packs)
    groups = [chains[g:g + group] for g in range(0, len(chains), group)]
    fillers = []
    for ch in groups[0]:
        prologue(*ch)
    for g, chs in enumerate(groups):
        if g + 1 < len(groups):
            fillers = [functools.partial(prologue, *ch) for ch in groups[g + 1]] + fillers
        fill = lambda: fillers.pop(0)() if fillers else None
        gram_issue(chs)
        gram_post(chs)
        fill()
        for lvl in range(1, n_lvl):
            level_first(chs, lvl)
            level_second(chs)
            level_post(chs, lvl)
            fill()
        solve_issue(chs)
        solve_post(chs)
        while fillers:
            fill()
        for c in sorted({c for c, _ in chs}):
            fillers += [functools.partial(rec_read, c), functools.partial(rec_inner, c),
                        lambda c=c: (rec_write(c), rec_out(c))]
    while fillers:
        fill()

    cbuf[0:tail, :] = cbuf[TB:TB + tail, :]
    cnew_ref[0] = cbuf[0:tail, :]

    @pl.when(t == pl.num_programs(1) - 1)
    def _():
        sfin_ref[0] = s_scr[...]


def _delta(proj3, ab3, conv_prev, s0, cw8, gate8, nw, C, NCH, HP):
    bsz, seqlen, _ = proj3.shape
    qkv_dim = cw8.shape[1]
    dn = qkv_dim // 3
    heads = dn // HEAD_DIM
    tb = NCH * C
    kern = functools.partial(_delta_kernel, C=C, NCH=NCH, HP=HP, H=heads)
    masks = _delta_masks(C, HP)
    const = lambda a: pl.BlockSpec(a.shape, lambda b, t: (0,) * a.ndim)
    return pl.pallas_call(
        kern,
        grid=(bsz, seqlen // tb),
        in_specs=[
            pl.BlockSpec((1, tb, qkv_dim), lambda b, t: (b, t, 0)),
            pl.BlockSpec((1, tb, dn), lambda b, t: (b, t, qkv_dim // dn)),
            pl.BlockSpec((1, tb, LANE), lambda b, t: (b, t, 0)),
            pl.BlockSpec((1, SUBLANE, qkv_dim), lambda b, t: (b, 0, 0)),
            pl.BlockSpec((1, dn, HEAD_DIM), lambda b, t: (b, 0, 0)),
            pl.BlockSpec((SUBLANE, qkv_dim), lambda b, t: (0, 0)),
            pl.BlockSpec((SUBLANE, LANE), lambda b, t: (0, 0)),
            pl.BlockSpec((1, HEAD_DIM), lambda b, t: (0, 0)),
            const(masks),
        ],
        out_specs=[
            pl.BlockSpec((1, tb, dn), lambda b, t: (b, t, 0)),
            pl.BlockSpec((1, SUBLANE, qkv_dim), lambda b, t: (b, 0, 0)),
            pl.BlockSpec((1, dn, HEAD_DIM), lambda b, t: (b, 0, 0)),
        ],
        out_shape=[
            jax.ShapeDtypeStruct((bsz, seqlen, dn), BF16),
            jax.ShapeDtypeStruct((bsz, SUBLANE, qkv_dim), F32),
            jax.ShapeDtypeStruct((bsz, dn, HEAD_DIM), F32),
        ],
        scratch_shapes=[pltpu.VMEM((SUBLANE + tb, qkv_dim), F32), pltpu.VMEM((dn, HEAD_DIM), F32)],
        compiler_params=pltpu.CompilerParams(
            dimension_semantics=("arbitrary", "arbitrary"), vmem_limit_bytes=VMEM_LIMIT),
        name="delta_rule",
    )(proj3, proj3, ab3, conv_prev, s0, cw8, gate8, nw, masks)


def _s5_disc_kernel(are_ref, aim_ref, logdt_ref, lre_ref, lim_ref, fre_ref, fim_ref):
    a_re = are_ref[...]
    a_im = aim_ref[...]
    dt = jnp.exp(logdt_ref[...])
    mag = jnp.exp(a_re * dt)
    lam_re = mag * jnp.cos(a_im * dt)
    lam_im = mag * jnp.sin(a_im * dt)
    den = a_re * a_re + a_im * a_im
    nr = lam_re - 1.0
    lre_ref[...] = lam_re
    lim_ref[...] = lam_im
    fre_ref[...] = (nr * a_re + lam_im * a_im) / den
    fim_ref[...] = (lam_im * a_re - nr * a_im) / den


def _s5_disc(a_re, a_im, log_dt):
    shp = jax.ShapeDtypeStruct(a_re.shape, F32)
    return pl.pallas_call(_s5_disc_kernel, out_shape=[shp] * 4, name="s5_disc")(
        a_re, a_im, log_dt.reshape(-1, 1))


def _s5_kernel(u_ref, x0_ref, bb_ref, cc_ref, lam_ref, d_ref, wg_ref, bg_ref,
               y_ref, xfin_ref, utm, xs, ytm, st, *, B, TB, KT, SW):
    t = pl.program_id(0)
    kw = 2 * SW
    cw = u_ref.shape[2] // KT

    @pl.when(t == 0)
    def _():
        st[...] = x0_ref[...]

    nl = u_ref.shape[2] // LANE
    for b in range(B):
        for j in range(nl):
            utm[j, pl.ds(b, TB, stride=B), :] = u_ref[b, :, j * LANE:(j + 1) * LANE]
    u_t = jnp.concatenate([utm[j] for j in range(nl)], axis=1)
    ub = u_t.astype(BF16)

    def input_map(kt):
        xs[:, kt * kw:(kt + 1) * kw] = jnp.dot(ub[:, kt * cw:(kt + 1) * cw], bb_ref[kt],
                                               preferred_element_type=F32)

    def recurrence(kt):
        half = SW // 2
        for hf in range(2):
            c_re = kt * kw + hf * half
            c_im = c_re + SW
            l_re = jnp.broadcast_to(lam_ref[kt:kt + 1, hf * half:(hf + 1) * half], (B, half))
            l_im = jnp.broadcast_to(lam_ref[KT + kt:KT + kt + 1, hf * half:(hf + 1) * half], (B, half))
            x_re = st[:, c_re:c_re + half]
            x_im = st[:, c_im:c_im + half]
            for tt in range(TB):
                rows = slice(tt * B, (tt + 1) * B)
                n_re = l_re * x_re - l_im * x_im + xs[rows, c_re:c_re + half]
                n_im = l_re * x_im + l_im * x_re + xs[rows, c_im:c_im + half]
                xs[rows, c_re:c_re + half] = n_re
                xs[rows, c_im:c_im + half] = n_im
                x_re, x_im = n_re, n_im
            st[:, c_re:c_re + half] = x_re
            st[:, c_im:c_im + half] = x_im

    def output_map(kt):
        return jnp.dot(xs[:, kt * kw:(kt + 1) * kw].astype(BF16), cc_ref[kt],
                       preferred_element_type=F32)

    input_map(0)
    ys = []
    for kt in range(KT):
        if kt + 1 < KT:
            input_map(kt + 1)
        recurrence(kt)
        cols = slice(kt * cw, (kt + 1) * cw)
        y = output_map(kt) + d_ref[:, cols] * u_t[:, cols]
        ys.append(0.5 * y * (1.0 + jnp.tanh(math.sqrt(2.0 / math.pi) * (y + 0.044715 * (y * y * y)))))
    y = jnp.concatenate(ys, axis=1)
    y = y * _sigmoid(jnp.dot(y.astype(BF16), wg_ref[...], preferred_element_type=F32) + bg_ref[...])
    for j in range(nl):
        ytm[j] = y[:, j * LANE:(j + 1) * LANE]
    for b in range(B):
        for j in range(nl):
            y_ref[b, :, j * LANE:(j + 1) * LANE] = ytm[j, pl.ds(b, TB, stride=B), :].astype(y_ref.dtype)

    @pl.when(t == pl.num_programs(0) - 1)
    def _():
        xfin_ref[...] = st[...]


def _s5(proj3, x0, bb, cc, lam, d_row, w_glu, b_glu, TB):
    bsz, seqlen, n = proj3.shape
    width = d_row.shape[1]
    kt, cw, kw = bb.shape
    sdim = x0.shape[1]
    const = lambda shape: pl.BlockSpec(shape, lambda t: (0,) * len(shape))
    kern = functools.partial(_s5_kernel, B=bsz, TB=TB, KT=kt, SW=kw // 2)
    return pl.pallas_call(
        kern,
        grid=(seqlen // TB,),
        in_specs=[
            pl.BlockSpec((bsz, TB, width), lambda t: (0, t, n // width - 1)),
            const((bsz, sdim)), const((kt, cw, kw)), const((kt, kw, cw)), const(lam.shape),
            const((1, width)), const((width, width)), const((1, width)),
        ],
        out_specs=[pl.BlockSpec((bsz, TB, width), lambda t: (0, t, 0)), const((bsz, sdim))],
        out_shape=[jax.ShapeDtypeStruct((bsz, seqlen, width), BF16),
                   jax.ShapeDtypeStruct((bsz, sdim), F32)],
        scratch_shapes=[pltpu.VMEM((width // LANE, bsz * TB, LANE), F32),
                        pltpu.VMEM((bsz * TB, sdim), F32),
                        pltpu.VMEM((width // LANE, bsz * TB, LANE), F32),
                        pltpu.VMEM((bsz, sdim), F32)],
        compiler_params=pltpu.CompilerParams(
            dimension_semantics=("arbitrary",), vmem_limit_bytes=VMEM_LIMIT),
        name="s5",
    )(proj3, x0, bb, cc, lam, d_row, w_glu, b_glu)


def _out_ln_kernel(x_ref, o_ref, y5_ref, lng_ref, lnb_ref, wout_ref, g1_ref, b1_ref, x1_ref,
                   *, alpha):
    dn = o_ref.shape[1]
    h = _layer_norm(x_ref[...], lng_ref[...], lnb_ref[...])
    mix = (jnp.dot(o_ref[...], wout_ref[0:dn, :], preferred_element_type=F32)
           + jnp.dot(y5_ref[...], wout_ref[dn:, :], preferred_element_type=F32))
    x1_ref[...] = _layer_norm(alpha * h + mix, g1_ref[...], b1_ref[...])


def _out_ln(x2, o2, y52, ln_g, ln_b, w_out, g1, b1, alpha, tm):
    rows, d = x2.shape
    row = lambda w: pl.BlockSpec((tm, w), lambda i: (i, 0))
    vec = pl.BlockSpec((1, d), lambda i: (0, 0))
    return pl.pallas_call(
        functools.partial(_out_ln_kernel, alpha=alpha),
        grid=(rows // tm,),
        in_specs=[row(d), row(o2.shape[1]), row(y52.shape[1]), vec, vec,
                  pl.BlockSpec(w_out.shape, lambda i: (0, 0)), vec, vec],
        out_specs=row(d),
        out_shape=jax.ShapeDtypeStruct((rows, d), F32),
        compiler_params=pltpu.CompilerParams(
            dimension_semantics=("arbitrary",), vmem_limit_bytes=VMEM_LIMIT),
        name="out_ln",
    )(x2, o2, y52, ln_g, ln_b, w_out, g1, b1)


def _ffn_kernel(x1_ref, wg_ref, wu_ref, wd_ref, g2_ref, b2_ref, y_ref, x1b, *, alpha):
    j = pl.program_id(1)

    @pl.when(j == 0)
    def _():
        x1 = x1_ref[...]
        x1b[...] = x1.astype(BF16)
        y_ref[...] = alpha * x1

    xb = x1b[...]
    gate = jnp.dot(xb, wg_ref[...], preferred_element_type=F32)
    up = jnp.dot(xb, wu_ref[...], preferred_element_type=F32)
    hb = (_silu(gate) * up).astype(BF16)
    th = wd_ref.shape[0]
    for c in range(0, y_ref.shape[1], th):
        y_ref[:, c:c + th] += jnp.dot(hb, wd_ref[:, c:c + th], preferred_element_type=F32)

    @pl.when(j == pl.num_programs(1) - 1)
    def _():
        y_ref[...] = _layer_norm(y_ref[...], g2_ref[...], b2_ref[...])


def _ffn(x1, w_gate, w_up, w_down, g2, b2, alpha, tm, th):
    rows, d = x1.shape
    fh = w_gate.shape[1]
    vec = pl.BlockSpec((1, d), lambda i, j: (0, 0))
    return pl.pallas_call(
        functools.partial(_ffn_kernel, alpha=alpha),
        grid=(rows // tm, fh // th),
        in_specs=[
            pl.BlockSpec((tm, d), lambda i, j: (i, 0)),
            pl.BlockSpec((d, th), lambda i, j: (0, j)),
            pl.BlockSpec((d, th), lambda i, j: (0, j)),
            pl.BlockSpec((th, d), lambda i, j: (j, 0)),
            vec, vec,
        ],
        out_specs=pl.BlockSpec((tm, d), lambda i, j: (i, 0)),
        out_shape=jax.ShapeDtypeStruct((rows, d), F32),
        scratch_shapes=[pltpu.VMEM((tm, d), BF16)],
        compiler_params=pltpu.CompilerParams(
            dimension_semantics=("arbitrary", "arbitrary"), vmem_limit_bytes=FFN_VMEM_LIMIT),
        name="ffn",
    )(x1, w_gate, w_up, w_down, g2, b2)


def _largest_divisor(n, cap, mult):
    best = mult
    for c in range(mult, cap + 1, mult):
        if n % c == 0:
            best = c
    return best


def _layer(x3, conv_prev8, s0, x0, prm, chunk, chunks_per_step, heads_per_pack, s5_tb):
    bsz, seqlen, d = x3.shape
    rows = bsz * seqlen
    x2 = x3.reshape(rows, d)
    tm = _largest_divisor(rows, 1024, 16)
    proj, ab = _ln_inproj(x2, prm["ln_in_g"], prm["ln_in_b"], prm["w_qkvz"], prm["w_u"], prm["w_ab"], tm)
    proj3 = proj.reshape(bsz, seqlen, -1)
    o, conv_new, s_new = _delta(proj3, ab.reshape(bsz, seqlen, LANE), conv_prev8, s0,
                                prm["cw8"], prm["gate8"], prm["nw"], chunk, chunks_per_step,
                                heads_per_pack)
    y5, x_new = _s5(proj3, x0, prm["bb"], prm["cc"], prm["lam"], prm["s5_d"], prm["w_glu"],
                    prm["b_glu"], s5_tb)
    x1 = _out_ln(x2, o.reshape(rows, -1), y5.reshape(rows, -1), prm["ln_in_g"], prm["ln_in_b"],
                 prm["w_out"], prm["ln1_g"], prm["ln1_b"], prm["alpha"],
                 _largest_divisor(rows, 512, 16))
    y = _ffn(x1, prm["w_gate"], prm["w_up"], prm["w_down"], prm["ln2_g"], prm["ln2_b"],
             prm["alpha"], tm, _largest_divisor(prm["w_gate"].shape[1], 512, LANE))
    return y.reshape(bsz, seqlen, d), conv_new, s_new, x_new


def kernel(x_prompt, x_sample, state_conv_qkv, state_delta, state_s5_re, state_s5_im, meta_tokens, ln_in_g, ln_in_b, w_in, conv_w, dn_a_log, dn_dt_bias, dn_norm_w, s5_a_re, s5_a_im, s5_log_dt, s5_b_re, s5_b_im, s5_c_re, s5_c_im, s5_d, s5_w_glu, s5_b_glu, w_out, ln1_g, ln1_b, ffn_w_gate, ffn_w_up, ffn_w_down, ln2_g, ln2_b):
    depth = w_in.shape[0]
    assert depth == 1, "single-layer trunk only"
    bp, seq, d = x_prompt.shape
    bs, dec_seq, _ = x_sample.shape
    n_meta = meta_tokens.shape[0]
    qkv_dim = conv_w.shape[-1]
    dn = qkv_dim // 3
    heads = dn // HEAD_DIM
    groups, sdim = s5_a_re.shape[1:]
    width = s5_d.shape[-1]
    assert n_meta == dec_seq and seq % CHUNK == 0 and width == groups * S5_GROUP_CH
    alpha = (2.0 * depth) ** 0.25
    row = lambda v: v.reshape(1, -1).astype(F32)

    w = w_in[0]
    n_gate = 2 * heads
    w_qkvz = w[:, :qkv_dim + dn].astype(BF16)
    w_u = w[:, qkv_dim + dn + n_gate:].astype(BF16)
    w_ab = jnp.pad(w[:, qkv_dim + dn:qkv_dim + dn + n_gate], ((0, 0), (0, LANE - n_gate))).astype(BF16)
    gate8 = jnp.zeros((SUBLANE, LANE), F32)
    gate8 = gate8.at[0, :heads].set(dn_a_log[0]).at[1, :heads].set(dn_dt_bias[0])
    cw8 = jnp.pad(conv_w[0].astype(F32), ((0, SUBLANE - CONV_W), (0, 0)))

    lam_re, lam_im, f_re, f_im = _s5_disc(s5_a_re[0], s5_a_im[0], s5_log_dt[0])
    bb_re = f_re[..., None] * s5_b_re[0] - f_im[..., None] * s5_b_im[0]
    bb_im = f_re[..., None] * s5_b_im[0] + f_im[..., None] * s5_b_re[0]
    gk = GROUPS_PER_KTILE
    kt = groups // gk
    eye = jnp.eye(gk, dtype=F32)
    bb5 = jnp.stack([bb_re, bb_im]).reshape(2, kt, gk, sdim, S5_GROUP_CH)
    bb = jnp.einsum("rkgph,gG->kGhrgp", bb5, eye).reshape(kt, gk * S5_GROUP_CH, 2 * gk * sdim).astype(BF16)
    cc5 = jnp.stack([s5_c_re[0], -s5_c_im[0]]).reshape(2, kt, gk, S5_GROUP_CH, sdim)
    cc = jnp.einsum("rkghp,gG->krgpGh", cc5, eye).reshape(kt, 2 * gk * sdim, gk * S5_GROUP_CH).astype(BF16)
    lam = jnp.concatenate([lam_re.reshape(kt, gk * sdim), lam_im.reshape(kt, gk * sdim)], axis=0)

    prm = dict(
        ln_in_g=row(ln_in_g), ln_in_b=row(ln_in_b), w_qkvz=w_qkvz, w_u=w_u, w_ab=w_ab, cw8=cw8, gate8=gate8,
        nw=row(dn_norm_w[0]), bb=bb, cc=cc, lam=lam, s5_d=row(s5_d[0]), w_glu=s5_w_glu[0].astype(BF16),
        b_glu=row(s5_b_glu[0]), w_out=w_out[0].astype(BF16), ln1_g=row(ln1_g[0]), ln1_b=row(ln1_b[0]),
        w_gate=ffn_w_gate[0].astype(BF16), w_up=ffn_w_up[0].astype(BF16),
        w_down=ffn_w_down[0].astype(BF16), ln2_g=row(ln2_g[0]), ln2_b=row(ln2_b[0]), alpha=alpha)

    def s5_pack(re, im):
        b = re.shape[0]
        return jnp.stack([re.reshape(b, kt, gk, sdim), im.reshape(b, kt, gk, sdim)], axis=2).reshape(b, -1)

    def s5_unpack(x):
        b = x.shape[0]
        x = x.reshape(b, kt, 2, gk, sdim)
        return x[:, :, 0].reshape(b, groups, sdim), x[:, :, 1].reshape(b, groups, sdim)

    nb = 2 * SUBLANE
    assert bs + 1 <= nb
    pad_b = lambda a: jnp.pad(a.astype(F32), ((0, nb - a.shape[0]),) + ((0, 0),) * (a.ndim - 1))
    x_small = pad_b(jnp.concatenate([x_sample, meta_tokens[None].astype(x_sample.dtype)], axis=0))
    conv_small = pad_b(jnp.pad(state_conv_qkv[0], ((0, 0), (SUBLANE - CONV_W + 1, 0), (0, 0))))
    s0_small = pad_b(state_delta[0].reshape(bs, dn, HEAD_DIM))
    x0_small = pad_b(s5_pack(state_s5_re[0], state_s5_im[0]))
    y_s, conv_s, s_s, xf_s = _layer(x_small, conv_small, s0_small, x0_small, prm,
                                    dec_seq, 1, heads, dec_seq)

    bc = lambda a: jnp.broadcast_to(a[bs:bs + 1], (bp,) + a.shape[1:])
    y_p, conv_p, s_p, xf_p = _layer(x_prompt, bc(conv_s), bc(s_s), bc(xf_s), prm,
                                    CHUNK, DELTA_CHUNKS_PER_STEP, DELTA_PACK_ROWS // CHUNK, CHUNK)

    re_p, im_p = s5_unpack(xf_p)
    re_s, im_s = s5_unpack(xf_s[:bs])
    tail = SUBLANE - CONV_W + 1
    return (y_p.astype(x_prompt.dtype), y_s[:bs].astype(x_sample.dtype),
            conv_p[None, :, tail:].astype(state_conv_qkv.dtype),
            s_p.reshape(bp, heads, HEAD_DIM, HEAD_DIM)[None].astype(state_delta.dtype),
            re_p[None].astype(state_s5_re.dtype), im_p[None].astype(state_s5_im.dtype),
            conv_s[None, :bs, tail:].astype(state_conv_qkv.dtype),
            s_s[:bs].reshape(bs, heads, HEAD_DIM, HEAD_DIM)[None].astype(state_delta.dtype),
            re_s[None].astype(state_s5_re.dtype), im_s[None].astype(state_s5_im.dtype))
```
